```python
import jax, jax.numpy as jnp
from jax import lax
import numpy as np

D_MODEL = 2048
BATCH = 1
SEQ = 8192
DEPTH = 2
DEC_BATCH = 8
DEC_SEQ = 32
PAST_LEN = 1024

CHUNK = 64
Q_BLOCK = 128
N_AB = (DEPTH + 1) // 2
N_SSD = DEPTH // 2
RMS_EPS = 1e-6
D_FF = 5632

GLA_HEADS = 4
GLA_DK = 128
GLA_DV = 256
GLA_GATE_RANK = 16
GLA_TAU = 16.0
MLA_HEADS = 8
MLA_Q_LORA = 512
MLA_KV_LORA = 512
MLA_NOPE = 128
MLA_ROPE = 64
MLA_V = 128
ROPE_THETA = 10000.0
SSD_D_INNER = 2 * D_MODEL
SSD_HEADDIM = 64
SSD_HEADS = SSD_D_INNER // SSD_HEADDIM
SSD_GROUPS = 8
SSD_REP = SSD_HEADS // SSD_GROUPS
SSD_N = 128
SSD_CONV = 4
SSD_CONV_CH = SSD_D_INNER + 2 * SSD_GROUPS * SSD_N

GLA_QK = GLA_HEADS * GLA_DK
GLA_VW = GLA_HEADS * GLA_DV
MLA_VW = MLA_HEADS * MLA_V
MLA_QW = MLA_HEADS * (MLA_NOPE + MLA_ROPE)
AB_IN = 2 * GLA_QK + 2 * GLA_VW + GLA_GATE_RANK + MLA_Q_LORA + MLA_KV_LORA + MLA_ROPE
AB_OUT = GLA_VW + MLA_VW
SSD_IN = SSD_D_INNER + SSD_CONV_CH + SSD_HEADS

kernel_name = 'hybrid_gla_mla_ssd_streaming_step'


def rmsnorm(x, g):
    xf = x.astype(jnp.float32)
    y = xf * lax.rsqrt(jnp.mean(xf * xf, axis=-1, keepdims=True) + RMS_EPS)
    return (y * g.astype(jnp.float32)).astype(x.dtype)


def swiglu(x, w_gate, w_up, w_down):
    return (jax.nn.silu(x @ w_gate) * (x @ w_up)) @ w_down


def rope(x, pos):
    half = MLA_ROPE // 2
    inv = ROPE_THETA ** (-jnp.arange(half, dtype=jnp.float32) / half)
    ang = pos.astype(jnp.float32)[:, None] * inv[None, :]
    shape = (ang.shape[0],) + (1,) * (x.ndim - 3) + (half,)
    cos = jnp.cos(ang).reshape(shape)
    sin = jnp.sin(ang).reshape(shape)
    x1 = x[..., :half].astype(jnp.float32)
    x2 = x[..., half:].astype(jnp.float32)
    return jnp.concatenate([x1 * cos - x2 * sin, x2 * cos + x1 * sin], axis=-1).astype(x.dtype)


def gla_recurrence(q, k, v, log_a, s0):
    B, T, H, DK = q.shape
    DV = v.shape[-1]
    L = min(CHUNK, T)
    n = T // L
    f32 = jnp.float32

    def blocks(a):
        return jnp.moveaxis(a.astype(f32).reshape((B, n, L) + a.shape[2:]), 1, 0)

    causal = jnp.tril(jnp.ones((L, L), dtype=bool))[None, :, :, None, None]

    def step(S, inp):
        qc, kc, vc, gc = inp
        b = jnp.cumsum(gc, axis=1)
        decay = jnp.exp(jnp.where(causal, b[:, :, None] - b[:, None, :], -jnp.inf))
        att = jnp.einsum('bthk,bshk,btshk->bhts', qc, kc, decay)
        o = jnp.einsum('bhts,bshv->bthv', att, vc) + jnp.einsum('bthk,bhkv->bthv', qc * jnp.exp(b), S)
        b_last = b[:, -1]
        S = S * jnp.exp(b_last)[..., None] + jnp.einsum('bshk,bshv->bhkv', kc * jnp.exp(b_last[:, None] - b), vc)
        return S, o

    S, o = lax.scan(step, s0.astype(f32), (blocks(q), blocks(k), blocks(v), blocks(log_a)))
    o = jnp.moveaxis(o, 0, 1).reshape(B, T, H, DV)
    return o.astype(v.dtype), S.astype(s0.dtype)


def ssd_recurrence(x, dt, a, bm, cm, s0):
    B, T = x.shape[:2]
    L = min(CHUNK, T)
    n = T // L
    f32 = jnp.float32

    def blocks(z):
        return jnp.moveaxis(z.astype(f32).reshape((B, n, L) + z.shape[2:]), 1, 0)

    causal = jnp.tril(jnp.ones((L, L), dtype=bool))[None, :, :, None, None]
    a32 = a.astype(f32)

    def step(S, inp):
        xc, dtc, bc, cc = inp
        cum = jnp.cumsum(dtc * a32, axis=1)
        lmat = jnp.exp(jnp.where(causal, cum[:, :, None] - cum[:, None, :], -jnp.inf))
        cb = jnp.einsum('btgn,bsgn->btsg', cc, bc)
        w = cb[..., None] * lmat * dtc[:, None]
        y = jnp.einsum('btsgr,bsgrp->btgrp', w, xc)
        y = y + jnp.einsum('btgn,bgrpn->btgrp', cc, S) * jnp.exp(cum)[..., None]
        last = cum[:, -1]
        wdec = jnp.exp(last[:, None] - cum) * dtc
        S = S * jnp.exp(last)[..., None, None] + jnp.einsum('bsgn,bsgrp->bgrpn', bc, xc * wdec[..., None])
        return S, y

    S0 = s0.astype(f32).reshape(B, SSD_GROUPS, SSD_REP, SSD_HEADDIM, SSD_N)
    S, y = lax.scan(step, S0, (blocks(x), blocks(dt), blocks(bm), blocks(cm)))
    y = jnp.moveaxis(y, 0, 1).reshape(x.shape)
    return y.astype(x.dtype), S.reshape(B, SSD_HEADS, SSD_HEADDIM, SSD_N).astype(s0.dtype)


def causal_dwconv(x, buf, w, b):
    T = x.shape[1]
    xp = jnp.concatenate([buf.astype(x.dtype), x], axis=1)
    y = b
    for j in range(SSD_CONV):
        y = y + xp[:, j:j + T] * w[j]
    return y, xp[:, -(SSD_CONV - 1):]


def mla_attention(q_nope, q_pe, k_nope, k_pe, v, q_pos, k_pos):
    B, Tq, H, _ = q_nope.shape
    blk = min(Q_BLOCK, Tq)
    n = Tq // blk
    scale = (MLA_NOPE + MLA_ROPE) ** -0.5
    k_chunk = k_pos // CHUNK

    def one_block(args):
        qn, qp, qpos = args
        s = jnp.einsum('bqhd,bkhd->bhqk', qn, k_nope) + jnp.einsum('bqhr,bkr->bhqk', qp, k_pe)
        s = s.astype(jnp.float32) * scale
        mask = k_chunk[None, :] <= (qpos // CHUNK)[:, None]
        p = jax.nn.softmax(jnp.where(mask[None, None], s, -jnp.inf), axis=-1).astype(v.dtype)
        return jnp.einsum('bhqk,bkhv->bqhv', p, v)

    qn_b = jnp.moveaxis(q_nope.reshape(B, n, blk, H, MLA_NOPE), 1, 0)
    qp_b = jnp.moveaxis(q_pe.reshape(B, n, blk, H, MLA_ROPE), 1, 0)
    out = lax.map(one_block, (qn_b, qp_b, q_pos.reshape(n, blk)))
    return jnp.moveaxis(out, 0, 1).reshape(B, Tq, H, MLA_V)


def ab_mixer(h, pos, gla_s0, ckv_past, kpe_past, w_in, w_gate_up, b_gate, g_gla, g_qn, w_uq, g_kvn, w_uk, w_uv, w_out):
    B, T, _ = h.shape
    sizes = (GLA_QK, GLA_QK, GLA_VW, GLA_GATE_RANK, GLA_VW, MLA_Q_LORA, MLA_KV_LORA, MLA_ROPE)
    offs = [int(o) for o in np.cumsum(sizes)[:-1]]
    q, k, v, gr, og, cq, ckv, kpe = jnp.split(h @ w_in, offs, axis=-1)
    q = q.reshape(B, T, GLA_HEADS, GLA_DK) * (GLA_DK ** -0.5)
    k = k.reshape(B, T, GLA_HEADS, GLA_DK)
    v = v.reshape(B, T, GLA_HEADS, GLA_DV)
    log_a = jax.nn.log_sigmoid((gr @ w_gate_up + b_gate).astype(jnp.float32)) / GLA_TAU
    log_a = log_a.reshape(B, T, GLA_HEADS, GLA_DK)
    o_gla, gla_s = gla_recurrence(q, k, v, log_a, gla_s0)
    o_gla = rmsnorm(o_gla, g_gla) * jax.nn.silu(og).reshape(B, T, GLA_HEADS, GLA_DV)
    qf = (rmsnorm(cq, g_qn) @ w_uq).reshape(B, T, MLA_HEADS, MLA_NOPE + MLA_ROPE)
    q_nope = qf[..., :MLA_NOPE]
    q_pe = rope(qf[..., MLA_NOPE:], pos)
    ckv_n = rmsnorm(ckv, g_kvn)
    kpe_r = rope(kpe, pos)
    if ckv_past is None:
        ckv_all, kpe_all, k_pos = ckv_n, kpe_r, pos
    else:
        ckv_all = jnp.concatenate([ckv_past.astype(ckv_n.dtype), ckv_n], axis=1)
        kpe_all = jnp.concatenate([kpe_past.astype(kpe_r.dtype), kpe_r], axis=1)
        k_pos = jnp.arange(ckv_all.shape[1], dtype=jnp.int32)
    Tk = ckv_all.shape[1]
    k_nope = (ckv_all @ w_uk).reshape(B, Tk, MLA_HEADS, MLA_NOPE)
    v_m = (ckv_all @ w_uv).reshape(B, Tk, MLA_HEADS, MLA_V)
    o_mla = mla_attention(q_nope, q_pe, k_nope, kpe_all, v_m, pos, k_pos)
    merged = jnp.concatenate([o_gla.reshape(B, T, GLA_VW), o_mla.reshape(B, T, MLA_VW)], axis=-1)
    return merged @ w_out, gla_s, ckv_n, kpe_r


def ssd_mixer(h, conv_buf, s0, w_in, w_conv, b_conv, dt_bias, a_log, d_skip, g_norm, w_out):
    B, T, _ = h.shape
    z, xbc, dt = jnp.split(h @ w_in, [SSD_D_INNER, SSD_D_INNER + SSD_CONV_CH], axis=-1)
    xbc, new_buf = causal_dwconv(xbc, conv_buf, w_conv, b_conv)
    xbc = jax.nn.silu(xbc)
    x, bm, cm = jnp.split(xbc, [SSD_D_INNER, SSD_D_INNER + SSD_GROUPS * SSD_N], axis=-1)
    x = x.reshape(B, T, SSD_GROUPS, SSD_REP, SSD_HEADDIM)
    bm = bm.reshape(B, T, SSD_GROUPS, SSD_N)
    cm = cm.reshape(B, T, SSD_GROUPS, SSD_N)
    dt = jax.nn.softplus((dt + dt_bias).astype(jnp.float32)).reshape(B, T, SSD_GROUPS, SSD_REP)
    a = -jnp.exp(a_log.astype(jnp.float32)).reshape(SSD_GROUPS, SSD_REP)
    y, s_new = ssd_recurrence(x, dt, a, bm, cm, s0)
    y = y + x * d_skip.reshape(SSD_GROUPS, SSD_REP)[..., None]
    y = (y.reshape(B, T, SSD_D_INNER) * jax.nn.silu(z)).reshape(B, T, SSD_GROUPS, SSD_D_INNER // SSD_GROUPS)
    y = rmsnorm(y, g_norm.reshape(SSD_GROUPS, SSD_D_INNER // SSD_GROUPS)).reshape(B, T, SSD_D_INNER)
    return y @ w_out, s_new, new_buf


def trunk(x, pos, past, P):
    B = x.shape[0]
    dt = x.dtype
    new_ckv, new_kpe, new_gla, new_ssd, new_conv = [], [], [], [], []
    for layer in range(DEPTH):
        h = rmsnorm(x, P['g_ffn1'][layer])
        x = x + 0.5 * swiglu(h, P['w_ffn1_gate'][layer], P['w_ffn1_up'][layer], P['w_ffn1_down'][layer])
        h = rmsnorm(x, P['g_mix'][layer])
        i = layer // 2
        if layer % 2 == 0:
            if past is None:
                gla0 = jnp.zeros((B, GLA_HEADS, GLA_DK, GLA_DV), dt)
                ckv_p, kpe_p = None, None
            else:
                gla0, ckv_p, kpe_p = past['gla'][i], past['ckv'][i], past['kpe'][i]
            out, gla_s, ckv_n, kpe_r = ab_mixer(
                h, pos, gla0, ckv_p, kpe_p, P['w_ab_in'][i], P['w_gla_gate_up'][i], P['b_gla_gate'][i],
                P['g_gla_norm'][i], P['g_mla_q_norm'][i], P['w_mla_uq'][i], P['g_mla_kv_norm'][i],
                P['w_mla_uk'][i], P['w_mla_uv'][i], P['w_ab_out'][i])
            new_gla.append(gla_s)
            new_ckv.append(ckv_n)
            new_kpe.append(kpe_r)
        else:
            if past is None:
                ssd0 = jnp.zeros((B, SSD_HEADS, SSD_HEADDIM, SSD_N), dt)
                buf0 = jnp.zeros((B, SSD_CONV - 1, SSD_CONV_CH), dt)
            else:
                ssd0, buf0 = past['ssd'][i], past['conv'][i]
            out, ssd_s, buf = ssd_mixer(
                h, buf0, ssd0, P['w_ssd_in'][i], P['w_ssd_conv'][i], P['b_ssd_conv'][i], P['ssd_dt_bias'][i],
                P['ssd_a_log'][i], P['ssd_d'][i], P['g_ssd_norm'][i], P['w_ssd_out'][i])
            new_ssd.append(ssd_s)
            new_conv.append(buf)
        x = x + out
        h = rmsnorm(x, P['g_ffn2'][layer])
        x = x + 0.5 * swiglu(h, P['w_ffn2_gate'][layer], P['w_ffn2_up'][layer], P['w_ffn2_down'][layer])
    y = rmsnorm(x, P['g_final'])
    return y, jnp.stack(new_ckv), jnp.stack(new_kpe), jnp.stack(new_gla), jnp.stack(new_ssd), jnp.stack(new_conv)


def setup_inputs(seed: int = 0) -> dict:
    key = jax.random.key(seed)
    ks = iter(jax.random.split(key, 64))
    f32 = jnp.float32

    def nrm(shape, scale):
        return jax.random.normal(next(ks), shape, f32) * scale

    def gain(shape):
        return 1.0 + 0.02 * jax.random.normal(next(ks), shape, f32)

    d = D_MODEL
    dt0 = jnp.exp(jax.random.uniform(next(ks), (N_SSD, SSD_HEADS), f32, np.log(1e-3), np.log(1e-1)))
    return {
        'x_prompt': nrm((BATCH, SEQ, d), 1.0),
        'x_sample': nrm((DEC_BATCH, DEC_SEQ, d), 1.0),
        'cache_mla_ckv': nrm((N_AB, DEC_BATCH, PAST_LEN, MLA_KV_LORA), 1.0),
        'cache_mla_kpe': nrm((N_AB, DEC_BATCH, PAST_LEN, MLA_ROPE), 1.0),
        'state_gla': nrm((N_AB, DEC_BATCH, GLA_HEADS, GLA_DK, GLA_DV), 0.1),
        'state_ssd': nrm((N_SSD, DEC_BATCH, SSD_HEADS, SSD_HEADDIM, SSD_N), 0.1),
        'state_ssd_conv': nrm((N_SSD, DEC_BATCH, SSD_CONV - 1, SSD_CONV_CH), 1.0),
        'g_ffn1': gain((DEPTH, d)),
        'w_ffn1_gate': nrm((DEPTH, d, D_FF), d ** -0.5),
        'w_ffn1_up': nrm((DEPTH, d, D_FF), d ** -0.5),
        'w_ffn1_down': nrm((DEPTH, D_FF, d), D_FF ** -0.5),
        'g_mix': gain((DEPTH, d)),
        'g_ffn2': gain((DEPTH, d)),
        'w_ffn2_gate': nrm((DEPTH, d, D_FF), d ** -0.5),
        'w_ffn2_up': nrm((DEPTH, d, D_FF), d ** -0.5),
        'w_ffn2_down': nrm((DEPTH, D_FF, d), D_FF ** -0.5),
        'w_ab_in': nrm((N_AB, d, AB_IN), d ** -0.5),
        'w_gla_gate_up': nrm((N_AB, GLA_GATE_RANK, GLA_QK), GLA_GATE_RANK ** -0.5),
        'b_gla_gate': nrm((N_AB, GLA_QK), 0.1) + 2.0,
        'g_gla_norm': gain((N_AB, GLA_DV)),
        'g_mla_q_norm': gain((N_AB, MLA_Q_LORA)),
        'w_mla_uq': nrm((N_AB, MLA_Q_LORA, MLA_QW), MLA_Q_LORA ** -0.5),
        'g_mla_kv_norm': gain((N_AB, MLA_KV_LORA)),
        'w_mla_uk': nrm((N_AB, MLA_KV_LORA, MLA_HEADS * MLA_NOPE), MLA_KV_LORA ** -0.5),
        'w_mla_uv': nrm((N_AB, MLA_KV_LORA, MLA_VW), MLA_KV_LORA ** -0.5),
        'w_ab_out': nrm((N_AB, AB_OUT, d), AB_OUT ** -0.5),
        'w_ssd_in': nrm((N_SSD, d, SSD_IN), d ** -0.5),
        'w_ssd_conv': nrm((N_SSD, SSD_CONV, SSD_CONV_CH), SSD_CONV ** -0.5),
        'b_ssd_conv': nrm((N_SSD, SSD_CONV_CH), 0.02),
        'ssd_dt_bias': dt0 + jnp.log(-jnp.expm1(-dt0)),
        'ssd_a_log': jnp.log(jax.random.uniform(next(ks), (N_SSD, SSD_HEADS), f32, 1.0, 16.0)),
        'ssd_d': gain((N_SSD, SSD_HEADS)),
        'g_ssd_norm': gain((N_SSD, SSD_D_INNER)),
        'w_ssd_out': nrm((N_SSD, SSD_D_INNER, d), SSD_D_INNER ** -0.5),
        'g_final': gain((d,)),
    }


def reference(x_prompt, x_sample, cache_mla_ckv, cache_mla_kpe, state_gla, state_ssd, state_ssd_conv,
              g_ffn1, w_ffn1_gate, w_ffn1_up, w_ffn1_down, g_mix, g_ffn2, w_ffn2_gate, w_ffn2_up, w_ffn2_down,
              w_ab_in, w_gla_gate_up, b_gla_gate, g_gla_norm, g_mla_q_norm, w_mla_uq, g_mla_kv_norm,
              w_mla_uk, w_mla_uv, w_ab_out, w_ssd_in, w_ssd_conv, b_ssd_conv, ssd_dt_bias, ssd_a_log, ssd_d,
              g_ssd_norm, w_ssd_out, g_final):
    P = dict(g_ffn1=g_ffn1, w_ffn1_gate=w_ffn1_gate, w_ffn1_up=w_ffn1_up, w_ffn1_down=w_ffn1_down,
             g_mix=g_mix, g_ffn2=g_ffn2, w_ffn2_gate=w_ffn2_gate, w_ffn2_up=w_ffn2_up, w_ffn2_down=w_ffn2_down,
             w_ab_in=w_ab_in, w_gla_gate_up=w_gla_gate_up, b_gla_gate=b_gla_gate, g_gla_norm=g_gla_norm,
             g_mla_q_norm=g_mla_q_norm, w_mla_uq=w_mla_uq, g_mla_kv_norm=g_mla_kv_norm, w_mla_uk=w_mla_uk,
             w_mla_uv=w_mla_uv, w_ab_out=w_ab_out, w_ssd_in=w_ssd_in, w_ssd_conv=w_ssd_conv,
             b_ssd_conv=b_ssd_conv, ssd_dt_bias=ssd_dt_bias, ssd_a_log=ssd_a_log, ssd_d=ssd_d,
             g_ssd_norm=g_ssd_norm, w_ssd_out=w_ssd_out, g_final=g_final)
    pos_p = jnp.arange(x_prompt.shape[1], dtype=jnp.int32)
    y_prompt, p_ckv, p_kpe, p_gla, p_ssd, p_conv = trunk(x_prompt, pos_p, None, P)
    past = dict(ckv=cache_mla_ckv, kpe=cache_mla_kpe, gla=state_gla, ssd=state_ssd, conv=state_ssd_conv)
    pos_s = cache_mla_ckv.shape[2] + jnp.arange(x_sample.shape[1], dtype=jnp.int32)
    y_sample, s_ckv, s_kpe, s_gla, s_ssd, s_conv = trunk(x_sample, pos_s, past, P)
    return (y_prompt, y_sample, p_ckv, p_kpe, p_gla, p_ssd, p_conv, s_ckv, s_kpe, s_gla, s_ssd, s_conv)
```

```python
import functools

import jax
import jax.numpy as jnp
from jax import lax
from jax.experimental import pallas as pl
from jax.experimental.pallas import tpu as pltpu

F32 = jnp.float32
BF16 = jnp.bfloat16

RMS_EPS = 1e-6
CHUNK = 64

GLA_HEADS = 4
GLA_DK = 128
GLA_DV = 256
GLA_GATE_RANK = 16
GLA_TAU = 16.0
GLA_QK = GLA_HEADS * GLA_DK
GLA_VW = GLA_HEADS * GLA_DV

MLA_HEADS = 8
MLA_Q_LORA = 512
MLA_KV_LORA = 512
MLA_NOPE = 128
MLA_ROPE = 64
MLA_V = 128
ROPE_THETA = 10000.0
MLA_QKW = 256

SSD_HEADDIM = 64
SSD_GROUPS = 8
SSD_N = 128
SSD_CONV = 4
SSD_GW = 512
SSD_REP = SSD_GW // SSD_HEADDIM

LANE = 128
AB_Q, AB_K, AB_V, AB_OG, AB_CQ, AB_CKV, AB_KPE, AB_GR, AB_W = 0, 512, 1024, 2048, 3072, 3584, 4096, 4224, 4352

VMEM_LIMIT = 56 * 1024 * 1024


def _tile(n, target, align):
    best = None
    for t in range(align, min(n, target) + 1, align):
        if n % t == 0:
            best = t
    return best if best is not None else n


def _rms(x, g):
    return x * lax.rsqrt(jnp.mean(x * x, axis=-1, keepdims=True) + RMS_EPS) * g


def _silu(x):
    return x * jax.nn.sigmoid(x)


def _softplus(x):
    return jnp.maximum(x, 0.0) + jnp.log1p(jnp.exp(-jnp.abs(x)))


def _dot(a, b):
    return jnp.dot(a, b, preferred_element_type=F32)


def _dot_nt(a, b):
    return lax.dot_general(a, b, (((1,), (1,)), ((), ())), preferred_element_type=F32)


def _dot_tn(a, b):
    return lax.dot_general(a, b, (((0,), (0,)), ((), ())), preferred_element_type=F32)


def _split3(x):
    h = x.astype(BF16)
    r = x - h.astype(F32)
    m = r.astype(BF16)
    l = (r - m.astype(F32)).astype(BF16)
    return h, m, l


def _dot_exact_rhs(a_bf16, x):
    h, m, l = _split3(x)
    return _dot(a_bf16, h) + _dot(a_bf16, m) + _dot(a_bf16, l)


def _dot_exact_lhs(x, b_bf16):
    h, m, l = _split3(x)
    return _dot(h, b_bf16) + _dot(m, b_bf16) + _dot(l, b_bf16)


def _params(*sem):
    return pltpu.CompilerParams(dimension_semantics=sem, vmem_limit_bytes=VMEM_LIMIT)


def _ffn_body(x_ref, g_ref, wg_ref, wu_ref, wd_ref, gf_ref, o_ref, h_ref, *, final_norm):
    j = pl.program_id(1)

    @pl.when(j == 0)
    def _():
        x = x_ref[...]
        h_ref[...] = _rms(x, g_ref[...]).astype(BF16)
        o_ref[...] = x

    h = h_ref[...]
    a = _dot(h, wg_ref[...])
    b = _dot(h, wu_ref[...])
    act = (_silu(a) * b) * 0.5
    o_ref[...] += _dot(act.astype(BF16), wd_ref[...])

    if final_norm:
        @pl.when(j == pl.num_programs(1) - 1)
        def _():
            o_ref[...] = _rms(o_ref[...], gf_ref[...])


def _ffn(x, g, wg, wu, wd, g_final=None):
    rows, d = x.shape
    f = wg.shape[1]
    tm = _tile(rows, 768, 8)
    tf = _tile(f, 512, LANE)
    final = g_final is not None
    return pl.pallas_call(
        functools.partial(_ffn_body, final_norm=final),
        grid=(rows // tm, f // tf),
        in_specs=[
            pl.BlockSpec((tm, d), lambda i, j: (i, 0)),
            pl.BlockSpec((1, d), lambda i, j: (0, 0)),
            pl.BlockSpec((d, tf), lambda i, j: (0, j)),
            pl.BlockSpec((d, tf), lambda i, j: (0, j)),
            pl.BlockSpec((tf, d), lambda i, j: (j, 0)),
            pl.BlockSpec((1, d), lambda i, j: (0, 0)),
        ],
        out_specs=pl.BlockSpec((tm, d), lambda i, j: (i, 0)),
        out_shape=jax.ShapeDtypeStruct((rows, d), F32),
        scratch_shapes=[pltpu.VMEM((tm, d), BF16)],
        compiler_params=_params("parallel", "arbitrary"),
        name="ffn",
    )(x, g, wg, wu, wd, g_final if final else g)


def _norm_proj_body(x_ref, g_ref, w_ref, o_ref, h_ref):
    @pl.when(pl.program_id(1) == 0)
    def _():
        h_ref[...] = _rms(x_ref[...], g_ref[...]).astype(BF16)

    o_ref[...] = _dot(h_ref[...], w_ref[...])


def _norm_proj(x, g, w, tn):
    rows, d = x.shape
    n = w.shape[1]
    tm = _tile(rows, 768, 8)
    return pl.pallas_call(
        _norm_proj_body,
        grid=(rows // tm, n // tn),
        in_specs=[
            pl.BlockSpec((tm, d), lambda i, j: (i, 0)),
            pl.BlockSpec((1, d), lambda i, j: (0, 0)),
            pl.BlockSpec((d, tn), lambda i, j: (0, j)),
        ],
        out_specs=pl.BlockSpec((tm, tn), lambda i, j: (i, j)),
        out_shape=jax.ShapeDtypeStruct((rows, n), F32),
        scratch_shapes=[pltpu.VMEM((tm, d), BF16)],
        compiler_params=_params("parallel", "arbitrary"),
        name="norm_proj",
    )(x, g, w)


def _out_proj_body(*refs, n_in):
    x_ref = refs[0]
    o_ref = refs[1 + 2 * n_in]
    acc = x_ref[...]
    for i in range(n_in):
        acc = acc + _dot(refs[1 + i][...], refs[1 + n_in + i][...])
    o_ref[...] = acc


def _out_proj(x, ys, ws):
    rows, d = x.shape
    n_in = len(ys)
    tm = _tile(rows, 768, 8)
    tn = _tile(d, 1024, LANE)
    in_specs = [pl.BlockSpec((tm, tn), lambda i, j: (i, j))]
    in_specs += [pl.BlockSpec((tm, y.shape[1]), lambda i, j: (i, 0)) for y in ys]
    in_specs += [pl.BlockSpec((w.shape[0], tn), lambda i, j: (0, j)) for w in ws]
    return pl.pallas_call(
        functools.partial(_out_proj_body, n_in=n_in),
        grid=(rows // tm, d // tn),
        in_specs=in_specs,
        out_specs=pl.BlockSpec((tm, tn), lambda i, j: (i, j)),
        out_shape=jax.ShapeDtypeStruct((rows, d), F32),
        compiler_params=_params("parallel", "arbitrary"),
        name="out_proj",
    )(x, *ys, *ws)


def _gla_body(q_ref, k_ref, v_ref, og_ref, gr_ref, s0_ref, wg_ref, bg_ref, gn_ref, *rest, aliased):
    o_ref, s_out_ref, s_ref = rest[1:] if aliased else rest
    c = pl.program_id(2)
    L = q_ref.shape[0]

    @pl.when(c == 0)
    def _():
        s_ref[...] = s0_ref[0, 0].T

    q = q_ref[...] * (GLA_DK ** -0.5)
    k = k_ref[...]
    v = v_ref[...].astype(BF16)
    pre = _dot(gr_ref[...].astype(BF16), wg_ref[...]) + bg_ref[...]
    log_a = (jnp.minimum(pre, 0.0) - jnp.log1p(jnp.exp(-jnp.abs(pre)))) * (1.0 / GLA_TAU)

    row = lax.broadcasted_iota(jnp.int32, (L, L), 0)
    col = lax.broadcasted_iota(jnp.int32, (L, L), 1)
    causal = row >= col
    tri = jnp.where(causal, 1.0, 0.0).astype(BF16)
    b = _dot_exact_rhs(tri, log_a)
    b_mid = b[L // 2 - 1:L // 2, :]
    b_last = b[L - 1:L, :]

    q_in = (q * jnp.exp(b)).astype(BF16)
    q_mid = (q * jnp.exp(b - b_mid)).astype(BF16)
    k_mid = (k * jnp.exp(b_mid - b)).astype(BF16)
    k_end = (k * jnp.exp(b_last - b)).astype(BF16)

    s_old = s_ref[...]
    att = jnp.where(causal, _dot_nt(q_mid, k_mid), 0.0)
    o = _dot(att.astype(BF16), v) + _dot_nt(q_in, s_old.astype(BF16))
    s_new = s_old * jnp.exp(b_last) + _dot_tn(v, k_end)
    s_ref[...] = s_new

    o_ref[...] = (_rms(o, gn_ref[...]) * _silu(og_ref[...])).astype(o_ref.dtype)

    @pl.when(c == pl.num_programs(2) - 1)
    def _():
        s_out_ref[0, 0] = s_new.T


def _gla(proj, row0, n_seq, seq_len, chunk, s0, w_gate, b_gate, g_norm, dst=None):
    rows = proj.shape[0]
    L = chunk
    n_chunks = seq_len // L
    base = row0 // L

    def rmap(width_blocks_off):
        return lambda b, h, c: (base + b * n_chunks + c, width_blocks_off + h)

    in_specs = [
        pl.BlockSpec((L, GLA_DK), rmap(AB_Q // GLA_DK)),
        pl.BlockSpec((L, GLA_DK), rmap(AB_K // GLA_DK)),
        pl.BlockSpec((L, GLA_DV), rmap(AB_V // GLA_DV)),
        pl.BlockSpec((L, GLA_DV), rmap(AB_OG // GLA_DV)),
        pl.BlockSpec((L, LANE), lambda b, h, c: (base + b * n_chunks + c, AB_GR // LANE)),
        pl.BlockSpec((1, 1, GLA_DK, GLA_DV), lambda b, h, c: (b, h, 0, 0)),
        pl.BlockSpec((LANE, GLA_DK), lambda b, h, c: (0, h)),
        pl.BlockSpec((1, GLA_DK), lambda b, h, c: (0, h)),
        pl.BlockSpec((1, GLA_DV), lambda b, h, c: (0, 0)),
    ]
    args = [proj, proj, proj, proj, proj, s0, w_gate, b_gate, g_norm]
    aliases = {}
    if dst is not None:
        in_specs.append(pl.BlockSpec(memory_space=pl.ANY))
        args.append(dst)
        aliases = {len(args) - 1: 0}
    return pl.pallas_call(
        functools.partial(_gla_body, aliased=dst is not None),
        grid=(n_seq, GLA_HEADS, n_chunks),
        in_specs=in_specs,
        out_specs=[
            pl.BlockSpec((L, GLA_DV), lambda b, h, c: (base + b * n_chunks + c, h)),
            pl.BlockSpec((1, 1, GLA_DK, GLA_DV), lambda b, h, c: (b, h, 0, 0)),
        ],
        out_shape=[
            jax.ShapeDtypeStruct((rows, GLA_VW), BF16),
            jax.ShapeDtypeStruct((n_seq, GLA_HEADS, GLA_DK, GLA_DV), F32),
        ],
        scratch_shapes=[pltpu.VMEM((GLA_DV, GLA_DK), F32)],
        input_output_aliases=aliases,
        compiler_params=_params("parallel", "parallel", "arbitrary"),
        name="gla",
    )(*args)


def _rope128(x, cos_t, sin_t):
    return x * cos_t + pltpu.roll(x, MLA_ROPE, 1) * sin_t


def _mla_pre_body(cq_ref, ckv_ref, kpe_ref, cos_ref, sin_ref, gq_ref, gkv_ref, wuq_ref, wuk_ref, wuv_ref,
                  q_ref, k_ref, v_ref, ckvn_ref, kper_ref):
    scale = (MLA_NOPE + MLA_ROPE) ** -0.5
    cos_t = cos_ref[...]
    sin_t = sin_ref[...]
    qf = _dot(_rms(cq_ref[...], gq_ref[...]).astype(BF16), wuq_ref[...])
    ckv_n = _rms(ckv_ref[...], gkv_ref[...])
    ckvn_ref[...] = ckv_n
    kpe_r = _rope128(kpe_ref[...], cos_t, sin_t)
    kper_ref[...] = kpe_r
    kpe_b = kpe_r.astype(BF16)
    ckv_b = ckv_n.astype(BF16)
    k_nope = _dot(ckv_b, wuk_ref[...])
    v_ref[...] = _dot(ckv_b, wuv_ref[...]).astype(BF16)
    for h in range(MLA_HEADS):
        o = h * MLA_QKW
        q_ref[:, o:o + MLA_NOPE] = (qf[:, o:o + MLA_NOPE] * scale).astype(BF16)
        q_ref[:, o + MLA_NOPE:o + MLA_QKW] = (_rope128(qf[:, o + MLA_NOPE:o + MLA_QKW], cos_t, sin_t) * scale).astype(BF16)
        k_ref[:, o:o + MLA_NOPE] = k_nope[:, h * MLA_NOPE:(h + 1) * MLA_NOPE].astype(BF16)
        k_ref[:, o + MLA_NOPE:o + MLA_QKW] = kpe_b


def _mla_pre(proj, cos_t, sin_t, g_q, g_kv, w_uq, w_uk, w_uv):
    rows = proj.shape[0]
    tm = _tile(rows, 256, 16)
    qkw = MLA_HEADS * MLA_QKW
    vw = MLA_HEADS * MLA_V
    row = lambda i: (i, 0)
    const = lambda i: (0, 0)
    return pl.pallas_call(
        _mla_pre_body,
        grid=(rows // tm,),
        in_specs=[
            pl.BlockSpec((tm, MLA_Q_LORA), lambda i: (i, AB_CQ // MLA_Q_LORA)),
            pl.BlockSpec((tm, MLA_KV_LORA), lambda i: (i, AB_CKV // MLA_KV_LORA)),
            pl.BlockSpec((tm, LANE), lambda i: (i, AB_KPE // LANE)),
            pl.BlockSpec((tm, LANE), row),
            pl.BlockSpec((tm, LANE), row),
            pl.BlockSpec((1, MLA_Q_LORA), const),
            pl.BlockSpec((1, MLA_KV_LORA), const),
            pl.BlockSpec((MLA_Q_LORA, qkw), const),
            pl.BlockSpec((MLA_KV_LORA, MLA_HEADS * MLA_NOPE), const),
            pl.BlockSpec((MLA_KV_LORA, vw), const),
        ],
        out_specs=[
            pl.BlockSpec((tm, qkw), row),
            pl.BlockSpec((tm, qkw), row),
            pl.BlockSpec((tm, vw), row),
            pl.BlockSpec((tm, MLA_KV_LORA), row),
            pl.BlockSpec((tm, LANE), row),
        ],
        out_shape=[
            jax.ShapeDtypeStruct((rows, qkw), BF16),
            jax.ShapeDtypeStruct((rows, qkw), BF16),
            jax.ShapeDtypeStruct((rows, vw), BF16),
            jax.ShapeDtypeStruct((rows, MLA_KV_LORA), F32),
            jax.ShapeDtypeStruct((rows, LANE), F32),
        ],
        compiler_params=_params("parallel"),
        name="mla_pre",
    )(proj, proj, proj, cos_t, sin_t, g_q, g_kv, w_uq, w_uk, w_uv)


def _flash_body(q_ref, k_ref, v_ref, o_ref, m_ref, l_ref, acc_ref, *, tq, tk):
    qi = pl.program_id(1)
    ki = pl.program_id(2)

    @pl.when(ki == 0)
    def _():
        m_ref[...] = jnp.full(m_ref.shape, -jnp.inf, F32)
        l_ref[...] = jnp.zeros(l_ref.shape, F32)
        acc_ref[...] = jnp.zeros(acc_ref.shape, F32)

    last_q_chunk = (qi * tq + tq - 1) // CHUNK

    @pl.when((ki * tk) // CHUNK <= last_q_chunk)
    def _():
        s = _dot_nt(q_ref[...], k_ref[...])
        q_chunk = (qi * tq + lax.broadcasted_iota(jnp.int32, (tq, tk), 0)) // CHUNK
        k_chunk = (ki * tk + lax.broadcasted_iota(jnp.int32, (tq, tk), 1)) // CHUNK
        s = jnp.where(k_chunk <= q_chunk, s, -jnp.inf)
        m_old = m_ref[...]
        m_new = jnp.maximum(m_old, jnp.max(s, axis=-1, keepdims=True))
        alpha = jnp.exp(m_old - m_new)
        p = jnp.exp(s - m_new)
        l_ref[...] = alpha * l_ref[...] + jnp.sum(p, axis=-1, keepdims=True)
        acc_ref[...] = alpha * acc_ref[...] + _dot(p.astype(BF16), v_ref[...])
        m_ref[...] = m_new

    @pl.when(ki == pl.num_programs(2) - 1)
    def _():
        o_ref[...] = (acc_ref[...] / l_ref[...]).astype(o_ref.dtype)


def _flash(q, k, v, seq_len, tile):
    rows = q.shape[0]
    tq = tk = _tile(seq_len, tile, CHUNK)
    nq = seq_len // tq

    def kv_block(qi, ki):
        return jnp.minimum(ki, ((qi * tq + tq - 1) // CHUNK * CHUNK + CHUNK - 1) // tk)

    return pl.pallas_call(
        functools.partial(_flash_body, tq=tq, tk=tk),
        grid=(MLA_HEADS, nq, seq_len // tk),
        in_specs=[
            pl.BlockSpec((tq, MLA_QKW), lambda h, qi, ki: (qi, h)),
            pl.BlockSpec((tk, MLA_QKW), lambda h, qi, ki: (kv_block(qi, ki), h)),
            pl.BlockSpec((tk, MLA_V), lambda h, qi, ki: (kv_block(qi, ki), h)),
        ],
        out_specs=pl.BlockSpec((tq, MLA_V), lambda h, qi, ki: (qi, h)),
        out_shape=jax.ShapeDtypeStruct((rows, MLA_HEADS * MLA_V), BF16),
        scratch_shapes=[pltpu.VMEM((tq, 1), F32), pltpu.VMEM((tq, 1), F32), pltpu.VMEM((tq, MLA_V), F32)],
        compiler_params=_params("parallel", "parallel", "arbitrary"),
        name="mla_flash",
    )(q, k, v)


def _mla_hist_body(q_ref, kn_ref, vn_ref, ckv_ref, kpe_ref, wuk_ref, wuv_ref, dst_ref, o_ref, kpe_pad_ref, *, past_len):
    del dst_ref
    t_new = q_ref.shape[0]
    ckv_b = ckv_ref[0].astype(BF16)
    k_past = _dot(ckv_b, wuk_ref[...]).astype(BF16)
    v_past = _dot(ckv_b, wuv_ref[...]).astype(BF16)
    kpe_pad_ref[:, :MLA_ROPE] = kpe_ref[0].astype(BF16)
    kpe_pad_ref[:, MLA_ROPE:] = jnp.zeros((past_len, LANE - MLA_ROPE), BF16)
    kpe_past = kpe_pad_ref[...]

    q_chunk = (past_len + lax.broadcasted_iota(jnp.int32, (t_new, past_len), 0)) // CHUNK
    mask_past = lax.broadcasted_iota(jnp.int32, (t_new, past_len), 1) // CHUNK <= q_chunk
    q_chunk_n = (past_len + lax.broadcasted_iota(jnp.int32, (t_new, t_new), 0)) // CHUNK
    mask_new = (past_len + lax.broadcasted_iota(jnp.int32, (t_new, t_new), 1)) // CHUNK <= q_chunk_n

    for h in range(MLA_HEADS):
        o = h * MLA_QKW
        q_nope = q_ref[:, o:o + MLA_NOPE]
        q_pe = q_ref[:, o + MLA_NOPE:o + MLA_QKW]
        s_past = _dot_nt(q_nope, k_past[:, h * MLA_NOPE:(h + 1) * MLA_NOPE]) + _dot_nt(q_pe, kpe_past)
        s_new = _dot_nt(q_ref[:, o:o + MLA_QKW], kn_ref[:, o:o + MLA_QKW])
        s_past = jnp.where(mask_past, s_past, -jnp.inf)
        s_new = jnp.where(mask_new, s_new, -jnp.inf)
        m = jnp.maximum(jnp.max(s_past, axis=-1, keepdims=True), jnp.max(s_new, axis=-1, keepdims=True))
        p_past = jnp.exp(s_past - m)
        p_new = jnp.exp(s_new - m)
        denom = jnp.sum(p_past, axis=-1, keepdims=True) + jnp.sum(p_new, axis=-1, keepdims=True)
        pv = _dot(p_past.astype(BF16), v_past[:, h * MLA_V:(h + 1) * MLA_V])
        pv = pv + _dot(p_new.astype(BF16), vn_ref[:, h * MLA_V:(h + 1) * MLA_V])
        o_ref[:, h * MLA_V:(h + 1) * MLA_V] = (pv / denom).astype(o_ref.dtype)


def _mla_hist(q, k, v, row0, n_seq, t_new, ckv_past, kpe_past, w_uk, w_uv, dst):
    past_len = ckv_past.shape[1]
    base = row0 // t_new
    qkw = MLA_HEADS * MLA_QKW
    vw = MLA_HEADS * MLA_V
    row = lambda b: (base + b, 0)
    const = lambda b: (0, 0)
    return pl.pallas_call(
        functools.partial(_mla_hist_body, past_len=past_len),
        grid=(n_seq,),
        in_specs=[
            pl.BlockSpec((t_new, qkw), row),
            pl.BlockSpec((t_new, qkw), row),
            pl.BlockSpec((t_new, vw), row),
            pl.BlockSpec((1, past_len, MLA_KV_LORA), lambda b: (b, 0, 0)),
            pl.BlockSpec((1, past_len, MLA_ROPE), lambda b: (b, 0, 0)),
            pl.BlockSpec((MLA_KV_LORA, MLA_HEADS * MLA_NOPE), const),
            pl.BlockSpec((MLA_KV_LORA, vw), const),
            pl.BlockSpec(memory_space=pl.ANY),
        ],
        out_specs=pl.BlockSpec((t_new, vw), row),
        out_shape=jax.ShapeDtypeStruct(dst.shape, dst.dtype),
        scratch_shapes=[pltpu.VMEM((past_len, LANE), BF16)],
        input_output_aliases={7: 0},
        compiler_params=_params("parallel"),
        name="mla_hist",
    )(q, k, v, ckv_past, kpe_past, w_uk, w_uv, dst)


SSD_L = 128


def _conv_silu(x_ref, buf_ref, w_ref, b_ref, ext_ref, first):
    L = x_ref.shape[0]

    @pl.when(first)
    def _():
        ext_ref[8 - (SSD_CONV - 1):8, :] = buf_ref[0]

    @pl.when(jnp.logical_not(first))
    def _():
        ext_ref[0:8, :] = ext_ref[L:L + 8, :]

    ext_ref[8:8 + L, :] = x_ref[...]
    y = b_ref[...]
    for j in range(SSD_CONV):
        y = y + ext_ref[8 - (SSD_CONV - 1) + j:8 - (SSD_CONV - 1) + j + L, :] * w_ref[j:j + 1, :]
    return _silu(y)


def _pad_rows(x, n):
    if x.shape[0] == n:
        return x
    return jnp.concatenate([x, jnp.zeros((n - x.shape[0], x.shape[1]), x.dtype)], axis=0)


def _ssd_body(z_ref, x_ref, bm_ref, cm_ref, dt_ref, bufx_ref, bufb_ref, bufc_ref, wx_ref, wb_ref, wc_ref,
              bx_ref, bb_ref, bc_ref, dtb_ref, alog_ref, dsk_ref, gn_ref, s0_ref, *rest, aliased):
    y_ref, s_out_ref, s_ref, extx_ref, extb_ref, extc_ref = rest[1:] if aliased else rest
    c = pl.program_id(2)
    first = c == 0
    L = x_ref.shape[0]
    LS = SSD_L

    @pl.when(first)
    def _():
        s_ref[...] = s0_ref[0].reshape(SSD_GW, SSD_N).T

    xs = _conv_silu(x_ref, bufx_ref, wx_ref, bx_ref, extx_ref, first)
    bm = _conv_silu(bm_ref, bufb_ref, wb_ref, bb_ref, extb_ref, first)
    cm = _conv_silu(cm_ref, bufc_ref, wc_ref, bc_ref, extc_ref, first)

    dt = _softplus(dt_ref[...] + dtb_ref[0])
    xs_p = _pad_rows(xs, LS)
    bm_p = _pad_rows(bm, LS).astype(BF16)
    dt_p = _pad_rows(dt, LS)
    a_row = -jnp.exp(alog_ref[0])
    da = dt_p * a_row

    row = lax.broadcasted_iota(jnp.int32, (LS, LS), 0)
    col = lax.broadcasted_iota(jnp.int32, (LS, LS), 1)
    tri = jnp.where(row >= col, 1.0, 0.0).astype(BF16)
    cum = _dot_exact_rhs(tri, da)
    cum_t = cum.T
    dt_t = dt_p.T
    cum_q = cum[:L]
    last = cum[LS - 1:LS, :]

    erow = lax.broadcasted_iota(jnp.int32, (LANE, SSD_GW), 0)
    ecol = lax.broadcasted_iota(jnp.int32, (LANE, SSD_GW), 1) // SSD_HEADDIM
    expand = jnp.where(erow == ecol, 1.0, 0.0).astype(BF16)
    e_cum = _dot_exact_lhs(jnp.exp(cum_q), expand)
    w_dec = _dot_exact_lhs(jnp.exp(last - cum) * dt_p, expand)
    e_last = _dot_exact_lhs(jnp.exp(last), expand)

    cm_b = cm.astype(BF16)
    cb = _dot_nt(cm_b, bm_p)
    causal = lax.broadcasted_iota(jnp.int32, (L, LS), 0) >= lax.broadcasted_iota(jnp.int32, (L, LS), 1)
    lane_half = lax.broadcasted_iota(jnp.int32, (LS, LANE), 1) < SSD_HEADDIM

    def head_weights(r):
        diff = cum_q[:, r:r + 1] - cum_t[r:r + 1, :]
        return (cb * jnp.exp(jnp.where(causal, diff, -jnp.inf)) * dt_t[r:r + 1, :]).astype(BF16)

    y_parts = []
    for p in range(SSD_REP // 2):
        w_pair = jnp.concatenate([head_weights(2 * p), head_weights(2 * p + 1)], axis=1)
        x_pair = xs_p[:, p * LANE:(p + 1) * LANE]
        x_bd = jnp.concatenate([jnp.where(lane_half, x_pair, 0.0), jnp.where(lane_half, 0.0, x_pair)], axis=0)
        y_parts.append(_dot(w_pair, x_bd.astype(BF16)))
    y = jnp.concatenate(y_parts, axis=1)

    s_old = s_ref[...]
    y = y + _dot(cm_b, s_old.astype(BF16)) * e_cum
    s_new = s_old * e_last + _dot_tn(bm_p, (xs_p * w_dec).astype(BF16))
    s_ref[...] = s_new

    y = y + xs * dsk_ref[...]
    y = y * _silu(z_ref[...])
    y_ref[...] = _rms(y, gn_ref[...]).astype(y_ref.dtype)

    @pl.when(c == pl.num_programs(2) - 1)
    def _():
        s_out_ref[0] = s_new.T.reshape(SSD_REP, SSD_HEADDIM, SSD_N)


def _ssd(proj, row0, n_seq, seq_len, chunk, conv_buf, s0, w_conv, b_conv, dt_bias, a_log, d_skip, g_norm, dst=None):
    rows = proj.shape[0]
    d_inner = SSD_GROUPS * SSD_GW
    L = chunk
    n_chunks = seq_len // L
    base = row0 // L
    bn = SSD_GROUPS * SSD_N
    off_x, off_b, off_c, off_dt = d_inner, 2 * d_inner, 2 * d_inner + bn, 2 * d_inner + 2 * bn

    def rmap(col_blocks_off):
        return lambda b, g, c: (base + b * n_chunks + c, col_blocks_off + g)

    nb = SSD_CONV - 1
    in_specs = [
        pl.BlockSpec((L, SSD_GW), rmap(0)),
        pl.BlockSpec((L, SSD_GW), rmap(off_x // SSD_GW)),
        pl.BlockSpec((L, SSD_N), rmap(off_b // SSD_N)),
        pl.BlockSpec((L, SSD_N), rmap(off_c // SSD_N)),
        pl.BlockSpec((L, LANE), rmap(off_dt // LANE)),
        pl.BlockSpec((1, nb, SSD_GW), lambda b, g, c: (b, 0, g)),
        pl.BlockSpec((1, nb, SSD_N), lambda b, g, c: (b, 0, d_inner // SSD_N + g)),
        pl.BlockSpec((1, nb, SSD_N), lambda b, g, c: (b, 0, (d_inner + bn) // SSD_N + g)),
        pl.BlockSpec((SSD_CONV, SSD_GW), lambda b, g, c: (0, g)),
        pl.BlockSpec((SSD_CONV, SSD_N), lambda b, g, c: (0, d_inner // SSD_N + g)),
        pl.BlockSpec((SSD_CONV, SSD_N), lambda b, g, c: (0, (d_inner + bn) // SSD_N + g)),
        pl.BlockSpec((1, SSD_GW), lambda b, g, c: (0, g)),
        pl.BlockSpec((1, SSD_N), lambda b, g, c: (0, d_inner // SSD_N + g)),
        pl.BlockSpec((1, SSD_N), lambda b, g, c: (0, (d_inner + bn) // SSD_N + g)),
        pl.BlockSpec((1, 1, LANE), lambda b, g, c: (g, 0, 0)),
        pl.BlockSpec((1, 1, LANE), lambda b, g, c: (g, 0, 0)),
        pl.BlockSpec((1, SSD_GW), lambda b, g, c: (0, g)),
        pl.BlockSpec((1, SSD_GW), lambda b, g, c: (0, g)),
        pl.BlockSpec((1, SSD_REP, SSD_HEADDIM, SSD_N), lambda b, g, c: (b, g, 0, 0)),
    ]
    args = [proj] * 5 + [conv_buf] * 3 + [w_conv] * 3 + [b_conv] * 3 + [dt_bias, a_log, d_skip, g_norm, s0]
    aliases = {}
    if dst is not None:
        in_specs.append(pl.BlockSpec(memory_space=pl.ANY))
        args.append(dst)
        aliases = {len(args) - 1: 0}
    return pl.pallas_call(
        functools.partial(_ssd_body, aliased=dst is not None),
        grid=(n_seq, SSD_GROUPS, n_chunks),
        in_specs=in_specs,
        out_specs=[
            pl.BlockSpec((L, SSD_GW), lambda b, g, c: (base + b * n_chunks + c, g)),
            pl.BlockSpec((1, SSD_REP, SSD_HEADDIM, SSD_N), lambda b, g, c: (b, g, 0, 0)),
        ],
        out_shape=[
            jax.ShapeDtypeStruct((rows, d_inner), BF16),
            jax.ShapeDtypeStruct((n_seq, SSD_GROUPS * SSD_REP, SSD_HEADDIM, SSD_N), F32),
        ],
        scratch_shapes=[
            pltpu.VMEM((SSD_N, SSD_GW), F32),
            pltpu.VMEM((L + 8, SSD_GW), F32),
            pltpu.VMEM((L + 8, SSD_N), F32),
            pltpu.VMEM((L + 8, SSD_N), F32),
        ],
        input_output_aliases=aliases,
        compiler_params=_params("parallel", "parallel", "arbitrary"),
        name="ssd",
    )(*args)


def _rot_cols(w):
    half = w.shape[-1] // 2
    return jnp.concatenate([-w[..., half:], w[..., :half]], axis=-1)


def _prep_ab_in(w):
    d = w.shape[0]
    o = 0
    parts = {}
    for name, size in (("q", GLA_QK), ("k", GLA_QK), ("v", GLA_VW), ("gr", GLA_GATE_RANK), ("og", GLA_VW),
                       ("cq", MLA_Q_LORA), ("ckv", MLA_KV_LORA), ("kpe", MLA_ROPE)):
        parts[name] = w[:, o:o + size]
        o += size
    zeros = jnp.zeros((d, LANE - GLA_GATE_RANK), w.dtype)
    out = jnp.concatenate([parts["q"], parts["k"], parts["v"], parts["og"], parts["cq"], parts["ckv"],
                           parts["kpe"], _rot_cols(parts["kpe"]), parts["gr"], zeros], axis=1)
    assert out.shape[1] == AB_W
    return out.astype(BF16)


def _prep_uq(w):
    w = w.reshape(w.shape[0], MLA_HEADS, MLA_NOPE + MLA_ROPE)
    pe = w[..., MLA_NOPE:]
    out = jnp.concatenate([w[..., :MLA_NOPE], pe, _rot_cols(pe)], axis=-1)
    return out.reshape(w.shape[0], MLA_HEADS * MLA_QKW).astype(BF16)


def _prep_ssd_in(w, d_inner):
    d = w.shape[0]
    n_heads = SSD_GROUPS * SSD_REP
    main = w[:, :2 * d_inner + 2 * SSD_GROUPS * SSD_N]
    dt = w[:, 2 * d_inner + 2 * SSD_GROUPS * SSD_N:].reshape(d, SSD_GROUPS, SSD_REP)
    assert dt.shape[1] * dt.shape[2] == n_heads
    dt = jnp.concatenate([dt, jnp.zeros((d, SSD_GROUPS, LANE - SSD_REP), w.dtype)], axis=-1)
    return jnp.concatenate([main, dt.reshape(d, SSD_GROUPS * LANE)], axis=1).astype(BF16)


def _head_table(v):
    v = v.reshape(SSD_GROUPS, 1, SSD_REP)
    return jnp.concatenate([v, jnp.zeros((SSD_GROUPS, 1, LANE - SSD_REP), v.dtype)], axis=-1)


def _rope_tables(pos):
    half = MLA_ROPE // 2
    inv = ROPE_THETA ** (-jnp.arange(half, dtype=F32) / half)
    ang = pos.astype(F32)[:, None] * inv[None, :]
    zeros = jnp.zeros((pos.shape[0], LANE - MLA_ROPE), F32)
    cos = jnp.cos(ang)
    sin = jnp.sin(ang)
    return jnp.concatenate([cos, cos, zeros], axis=1), jnp.concatenate([sin, sin, zeros], axis=1)


def kernel(x_prompt, x_sample, cache_mla_ckv, cache_mla_kpe, state_gla, state_ssd, state_ssd_conv, g_ffn1, w_ffn1_gate, w_ffn1_up, w_ffn1_down, g_mix, g_ffn2, w_ffn2_gate, w_ffn2_up, w_ffn2_down, w_ab_in, w_gla_gate_up, b_gla_gate, g_gla_norm, g_mla_q_norm, w_mla_uq, g_mla_kv_norm, w_mla_uk, w_mla_uv, w_ab_out, w_ssd_in, w_ssd_conv, b_ssd_conv, ssd_dt_bias, ssd_a_log, ssd_d, g_ssd_norm, w_ssd_out, g_final):
    n_p, t_p, d = x_prompt.shape
    n_s, t_s, _ = x_sample.shape
    assert n_p == 1
    depth = g_ffn1.shape[0]
    past_len = cache_mla_ckv.shape[2]
    rows_p = n_p * t_p
    d_inner = w_ssd_out.shape[1]
    assert d_inner == SSD_GROUPS * SSD_GW

    x = jnp.concatenate([x_prompt.reshape(rows_p, d), x_sample.reshape(n_s * t_s, d)], axis=0)
    pos = jnp.concatenate([jnp.arange(t_p, dtype=jnp.int32),
                           jnp.tile(past_len + jnp.arange(t_s, dtype=jnp.int32), n_s)])
    cos_t, sin_t = _rope_tables(pos)
    row2 = lambda v: v.reshape(1, -1)

    outs = {k: [] for k in ("p_ckv", "p_kpe", "p_gla", "p_ssd", "p_conv", "s_ckv", "s_kpe", "s_gla", "s_ssd", "s_conv")}
    for layer in range(depth):
        i = layer // 2
        x = _ffn(x, row2(g_ffn1[layer]), w_ffn1_gate[layer].astype(BF16), w_ffn1_up[layer].astype(BF16),
                 w_ffn1_down[layer].astype(BF16))
        if layer % 2 == 0:
            proj = _norm_proj(x, row2(g_mix[layer]), _prep_ab_in(w_ab_in[i]), AB_W // 2)
            w_gate = jnp.concatenate([w_gla_gate_up[i], jnp.zeros((LANE - GLA_GATE_RANK, GLA_QK), F32)], axis=0).astype(BF16)
            gla_args = (w_gate, row2(b_gla_gate[i]), row2(g_gla_norm[i]))
            o_gla, p_gla = _gla(proj, 0, n_p, t_p, min(CHUNK, t_p),
                                jnp.zeros((n_p, GLA_HEADS, GLA_DK, GLA_DV), F32), *gla_args)
            o_gla, s_gla = _gla(proj, rows_p, n_s, t_s, t_s, state_gla[i], *gla_args, dst=o_gla)

            w_uk = w_mla_uk[i].astype(BF16)
            w_uv = w_mla_uv[i].astype(BF16)
            q, k, v, ckv_n, kpe_r = _mla_pre(proj, cos_t, sin_t, row2(g_mla_q_norm[i]), row2(g_mla_kv_norm[i]),
                                             _prep_uq(w_mla_uq[i]), w_uk, w_uv)
            o_mla = _flash(q, k, v, t_p, 512)
            o_mla = _mla_hist(q, k, v, rows_p, n_s, t_s, cache_mla_ckv[i], cache_mla_kpe[i], w_uk, w_uv, o_mla)

            w_out = w_ab_out[i].astype(BF16)
            x = _out_proj(x, [o_gla, o_mla], [w_out[:GLA_VW], w_out[GLA_VW:]])
            outs["p_ckv"].append(ckv_n[:rows_p].reshape(n_p, t_p, MLA_KV_LORA))
            outs["s_ckv"].append(ckv_n[rows_p:].reshape(n_s, t_s, MLA_KV_LORA))
            outs["p_kpe"].append(kpe_r[:rows_p, :MLA_ROPE].reshape(n_p, t_p, MLA_ROPE))
            outs["s_kpe"].append(kpe_r[rows_p:, :MLA_ROPE].reshape(n_s, t_s, MLA_ROPE))
            outs["p_gla"].append(p_gla)
            outs["s_gla"].append(s_gla)
        else:
            n_heads = SSD_GROUPS * SSD_REP
            conv_ch = d_inner + 2 * SSD_GROUPS * SSD_N
            proj = _norm_proj(x, row2(g_mix[layer]), _prep_ssd_in(w_ssd_in[i], d_inner),
                              (2 * d_inner + 2 * SSD_GROUPS * SSD_N + SSD_GROUPS * LANE) // 8)
            ssd_args = (w_ssd_conv[i], row2(b_ssd_conv[i]), _head_table(ssd_dt_bias[i]), _head_table(ssd_a_log[i]),
                        row2(jnp.repeat(ssd_d[i], SSD_HEADDIM)), row2(g_ssd_norm[i]))
            y, p_ssd = _ssd(proj, 0, n_p, t_p, min(SSD_L, t_p), jnp.zeros((n_p, SSD_CONV - 1, conv_ch), F32),
                            jnp.zeros((n_p, n_heads, SSD_HEADDIM, SSD_N), F32), *ssd_args)
            y, s_ssd = _ssd(proj, rows_p, n_s, t_s, t_s, state_ssd_conv[i], state_ssd[i], *ssd_args, dst=y)
            x = _out_proj(x, [y], [w_ssd_out[i].astype(BF16)])
            xbc = proj[:, d_inner:d_inner + conv_ch]
            outs["p_conv"].append(xbc[:rows_p].reshape(n_p, t_p, conv_ch)[:, t_p - (SSD_CONV - 1):])
            outs["s_conv"].append(xbc[rows_p:].reshape(n_s, t_s, conv_ch)[:, t_s - (SSD_CONV - 1):])
            outs["p_ssd"].append(p_ssd)
            outs["s_ssd"].append(s_ssd)
        x = _ffn(x, row2(g_ffn2[layer]), w_ffn2_gate[layer].astype(BF16), w_ffn2_up[layer].astype(BF16),
                 w_ffn2_down[layer].astype(BF16), g_final=row2(g_final) if layer == depth - 1 else None)

    y_prompt = x[:rows_p].reshape(n_p, t_p, d)
    y_sample = x[rows_p:].reshape(n_s, t_s, d)
    st = {k: jnp.stack(v) for k, v in outs.items()}
    return (y_prompt, y_sample, st["p_ckv"], st["p_kpe"], st["p_gla"], st["p_ssd"], st["p_conv"],
            st["s_ckv"], st["s_kpe"], st["s_gla"], st["s_ssd"], st["s_conv"])
```

```python
import functools

import jax
import jax.numpy as jnp
from jax import lax
from jax.experimental import pallas as pl
from jax.experimental.pallas import tpu as pltpu

F32 = jnp.float32
BF16 = jnp.bfloat16

RMS_EPS = 1e-6
CHUNK = 64

GLA_HEADS = 4
GLA_DK = 128
GLA_DV = 256
GLA_GATE_RANK = 16
GLA_TAU = 16.0
GLA_QK = GLA_HEADS * GLA_DK
GLA_VW = GLA_HEADS * GLA_DV

MLA_HEADS = 8
MLA_Q_LORA = 512
MLA_KV_LORA = 512
MLA_NOPE = 128
MLA_ROPE = 64
MLA_V = 128
ROPE_THETA = 10000.0
MLA_QKW = 256
MLA_SCORE_SCALE = (MLA_NOPE + MLA_ROPE) ** -0.5 * 1.4426950408889634

SSD_HEADDIM = 64
SSD_GROUPS = 8
SSD_N = 128
SSD_CONV = 4
SSD_GW = 512
SSD_REP = SSD_GW // SSD_HEADDIM

LANE = 128
AB_Q, AB_K, AB_V, AB_OG, AB_CQ, AB_CKV, AB_KPE, AB_GR, AB_W = 0, 512, 1024, 2048, 3072, 3584, 4096, 4224, 4352

VMEM_LIMIT = 56 * 1024 * 1024


def _tile(n, target, align):
    best = None
    for t in range(align, min(n, target) + 1, align):
        if n % t == 0:
            best = t
    return best if best is not None else n


def _rms(x, g):
    return x * lax.rsqrt(jnp.mean(x * x, axis=-1, keepdims=True) + RMS_EPS) * g


def _silu(x):
    return x * jax.nn.sigmoid(x)


def _softplus(x):
    return jnp.maximum(x, 0.0) + jnp.log1p(jnp.exp(-jnp.abs(x)))


def _dot(a, b):
    return jnp.dot(a, b, preferred_element_type=F32)


def _dot_nt(a, b):
    return lax.dot_general(a, b, (((1,), (1,)), ((), ())), preferred_element_type=F32)


def _dot_tn(a, b):
    return lax.dot_general(a, b, (((0,), (0,)), ((), ())), preferred_element_type=F32)


def _split3(x):
    h = x.astype(BF16)
    r = x - h.astype(F32)
    m = r.astype(BF16)
    l = (r - m.astype(F32)).astype(BF16)
    return h, m, l


def _dot_exact_rhs(a_bf16, x):
    h, m, l = _split3(x)
    return _dot(a_bf16, h) + _dot(a_bf16, m) + _dot(a_bf16, l)


def _dot_exact_lhs(x, b_bf16):
    h, m, l = _split3(x)
    return _dot(h, b_bf16) + _dot(m, b_bf16) + _dot(l, b_bf16)


def _params(*sem):
    return pltpu.CompilerParams(dimension_semantics=sem, vmem_limit_bytes=VMEM_LIMIT)


def _ffn_body(x_ref, g_ref, wg_ref, wu_ref, wd_ref, gf_ref, o_ref, h_ref, *, final_norm):
    j = pl.program_id(1)

    @pl.when(j == 0)
    def _():
        x = x_ref[...]
        h_ref[...] = _rms(x, g_ref[...]).astype(BF16)
        o_ref[...] = x

    h = h_ref[...]
    a = _dot(h, wg_ref[...])
    b = _dot(h, wu_ref[...])
    act = (_silu(a) * b) * 0.5
    o_ref[...] += _dot(act.astype(BF16), wd_ref[...])

    if final_norm:
        @pl.when(j == pl.num_programs(1) - 1)
        def _():
            o_ref[...] = _rms(o_ref[...], gf_ref[...])


def _ffn(x, g, wg, wu, wd, g_final=None):
    rows, d = x.shape
    f = wg.shape[1]
    tm = _tile(rows, 768, 8)
    tf = _tile(f, 512, LANE)
    final = g_final is not None
    return pl.pallas_call(
        functools.partial(_ffn_body, final_norm=final),
        grid=(rows // tm, f // tf),
        in_specs=[
            pl.BlockSpec((tm, d), lambda i, j: (i, 0)),
            pl.BlockSpec((1, d), lambda i, j: (0, 0)),
            pl.BlockSpec((d, tf), lambda i, j: (0, j)),
            pl.BlockSpec((d, tf), lambda i, j: (0, j)),
            pl.BlockSpec((tf, d), lambda i, j: (j, 0)),
            pl.BlockSpec((1, d), lambda i, j: (0, 0)),
        ],
        out_specs=pl.BlockSpec((tm, d), lambda i, j: (i, 0)),
        out_shape=jax.ShapeDtypeStruct((rows, d), F32),
        scratch_shapes=[pltpu.VMEM((tm, d), BF16)],
        compiler_params=_params("parallel", "arbitrary"),
        name="ffn",
    )(x, g, wg, wu, wd, g_final if final else g)


def _norm_proj_body(x_ref, g_ref, w_ref, o_ref, h_ref):
    @pl.when(pl.program_id(1) == 0)
    def _():
        h_ref[...] = _rms(x_ref[...], g_ref[...]).astype(BF16)

    o_ref[...] = _dot(h_ref[...], w_ref[...])


def _norm_proj(x, g, w, tn):
    rows, d = x.shape
    n = w.shape[1]
    tm = _tile(rows, 768, 8)
    return pl.pallas_call(
        _norm_proj_body,
        grid=(rows // tm, n // tn),
        in_specs=[
            pl.BlockSpec((tm, d), lambda i, j: (i, 0)),
            pl.BlockSpec((1, d), lambda i, j: (0, 0)),
            pl.BlockSpec((d, tn), lambda i, j: (0, j)),
        ],
        out_specs=pl.BlockSpec((tm, tn), lambda i, j: (i, j)),
        out_shape=jax.ShapeDtypeStruct((rows, n), F32),
        scratch_shapes=[pltpu.VMEM((tm, d), BF16)],
        compiler_params=_params("parallel", "arbitrary"),
        name="norm_proj",
    )(x, g, w)


def _out_proj_body(*refs, n_in):
    x_ref = refs[0]
    o_ref = refs[1 + 2 * n_in]
    acc = x_ref[...]
    for i in range(n_in):
        acc = acc + _dot(refs[1 + i][...], refs[1 + n_in + i][...])
    o_ref[...] = acc


def _out_proj(x, ys, ws):
    rows, d = x.shape
    n_in = len(ys)
    tm = _tile(rows, 768, 8)
    tn = _tile(d, 1024, LANE)
    in_specs = [pl.BlockSpec((tm, tn), lambda i, j: (i, j))]
    in_specs += [pl.BlockSpec((tm, y.shape[1]), lambda i, j: (i, 0)) for y in ys]
    in_specs += [pl.BlockSpec((w.shape[0], tn), lambda i, j: (0, j)) for w in ws]
    return pl.pallas_call(
        functools.partial(_out_proj_body, n_in=n_in),
        grid=(rows // tm, d // tn),
        in_specs=in_specs,
        out_specs=pl.BlockSpec((tm, tn), lambda i, j: (i, j)),
        out_shape=jax.ShapeDtypeStruct((rows, d), F32),
        compiler_params=_params("parallel", "arbitrary"),
        name="out_proj",
    )(x, *ys, *ws)


def _gla_body(q_ref, k_ref, v_ref, og_ref, gr_ref, s0_ref, wg_ref, bg_ref, gn_ref, *rest, aliased, L):
    o_ref, s_out_ref, s_ref = rest[1:] if aliased else rest
    step = pl.program_id(1)
    n_sub = q_ref.shape[0] // L

    @pl.when(step == 0)
    def _():
        for h in range(GLA_HEADS):
            s_ref[h] = s0_ref[0, h].T

    pre = _dot(gr_ref[...].astype(BF16), wg_ref[...]) + bg_ref[...]
    log_a = (jnp.minimum(pre, 0.0) - jnp.log1p(jnp.exp(-jnp.abs(pre)))) * (1.0 / GLA_TAU)
    causal = lax.broadcasted_iota(jnp.int32, (L, L), 0) >= lax.broadcasted_iota(jnp.int32, (L, L), 1)
    tri = jnp.where(causal, 1.0, 0.0).astype(BF16)
    gn = gn_ref[...]

    states = [s_ref[h] for h in range(GLA_HEADS)]
    for c in range(n_sub):
        r0 = c * L
        b_all = _dot_exact_rhs(tri, log_a[r0:r0 + L, :])
        for h in range(GLA_HEADS):
            kc = slice(h * GLA_DK, (h + 1) * GLA_DK)
            vc = slice(h * GLA_DV, (h + 1) * GLA_DV)
            b = b_all[:, kc]
            b_mid = b[L // 2 - 1:L // 2, :]
            b_last = b[L - 1:L, :]
            q = q_ref[r0:r0 + L, kc] * (GLA_DK ** -0.5)
            k = k_ref[r0:r0 + L, kc]
            v = v_ref[r0:r0 + L, vc].astype(BF16)
            q_in = (q * jnp.exp(b)).astype(BF16)
            q_mid = (q * jnp.exp(b - b_mid)).astype(BF16)
            k_mid = (k * jnp.exp(b_mid - b)).astype(BF16)
            k_end = (k * jnp.exp(b_last - b)).astype(BF16)
            s_old = states[h]
            att = jnp.where(causal, _dot_nt(q_mid, k_mid), 0.0)
            o = _dot(att.astype(BF16), v) + _dot_nt(q_in, s_old.astype(BF16))
            states[h] = s_old * jnp.exp(b_last) + _dot_tn(v, k_end)
            o_ref[r0:r0 + L, vc] = (_rms(o, gn) * _silu(og_ref[r0:r0 + L, vc])).astype(o_ref.dtype)

    for h in range(GLA_HEADS):
        s_ref[h] = states[h]

    @pl.when(step == pl.num_programs(1) - 1)
    def _():
        for h in range(GLA_HEADS):
            s_out_ref[0, h] = states[h].T


def _gla(proj, row0, n_seq, seq_len, chunk, n_sub, s0, w_gate, b_gate, g_norm, dst=None):
    rows = proj.shape[0]
    tr = chunk * n_sub
    n_steps = seq_len // tr
    base = row0 // tr

    def rmap(col_block):
        return lambda b, s: (base + b * n_steps + s, col_block)

    const = lambda b, s: (0, 0)
    in_specs = [
        pl.BlockSpec((tr, GLA_QK), rmap(AB_Q // GLA_QK)),
        pl.BlockSpec((tr, GLA_QK), rmap(AB_K // GLA_QK)),
        pl.BlockSpec((tr, GLA_VW), rmap(AB_V // GLA_VW)),
        pl.BlockSpec((tr, GLA_VW), rmap(AB_OG // GLA_VW)),
        pl.BlockSpec((tr, LANE), rmap(AB_GR // LANE)),
        pl.BlockSpec((1, GLA_HEADS, GLA_DK, GLA_DV), lambda b, s: (b, 0, 0, 0)),
        pl.BlockSpec((LANE, GLA_QK), const),
        pl.BlockSpec((1, GLA_QK), const),
        pl.BlockSpec((1, GLA_DV), const),
    ]
    args = [proj, proj, proj, proj, proj, s0, w_gate, b_gate, g_norm]
    aliases = {}
    if dst is not None:
        in_specs.append(pl.BlockSpec(memory_space=pl.ANY))
        args.append(dst)
        aliases = {len(args) - 1: 0}
    return pl.pallas_call(
        functools.partial(_gla_body, aliased=dst is not None, L=chunk),
        grid=(n_seq, n_steps),
        in_specs=in_specs,
        out_specs=[
            pl.BlockSpec((tr, GLA_VW), rmap(0)),
            pl.BlockSpec((1, GLA_HEADS, GLA_DK, GLA_DV), lambda b, s: (b, 0, 0, 0)),
        ],
        out_shape=[
            jax.ShapeDtypeStruct((rows, GLA_VW), BF16),
            jax.ShapeDtypeStruct((n_seq, GLA_HEADS, GLA_DK, GLA_DV), F32),
        ],
        scratch_shapes=[pltpu.VMEM((GLA_HEADS, GLA_DV, GLA_DK), F32)],
        input_output_aliases=aliases,
        compiler_params=_params("parallel", "arbitrary"),
        name="gla",
    )(*args)


def _rope128(x, cos_t, sin_t):
    return x * cos_t + pltpu.roll(x, MLA_ROPE, 1) * sin_t


def _mla_pre_body(cq_ref, ckv_ref, kpe_ref, cos_ref, sin_ref, gq_ref, gkv_ref, wuq_ref, wuk_ref, wuvt_ref,
                  q_ref, k_ref, vt_ref, ckvn_ref, kper_ref):
    scale = MLA_SCORE_SCALE
    cos_t = cos_ref[...]
    sin_t = sin_ref[...]
    qf = _dot(_rms(cq_ref[...], gq_ref[...]).astype(BF16), wuq_ref[...])
    ckv_n = _rms(ckv_ref[...], gkv_ref[...])
    ckvn_ref[...] = ckv_n
    kpe_r = _rope128(kpe_ref[...], cos_t, sin_t)
    kper_ref[...] = kpe_r
    kpe_b = kpe_r.astype(BF16)
    ckv_b = ckv_n.astype(BF16)
    k_nope = _dot(ckv_b, wuk_ref[...])
    vt_ref[...] = _dot_nt(wuvt_ref[...], ckv_b).astype(BF16)
    for h in range(MLA_HEADS):
        o = h * MLA_QKW
        q_ref[:, o:o + MLA_NOPE] = (qf[:, o:o + MLA_NOPE] * scale).astype(BF16)
        q_ref[:, o + MLA_NOPE:o + MLA_QKW] = (_rope128(qf[:, o + MLA_NOPE:o + MLA_QKW], cos_t, sin_t) * scale).astype(BF16)
        k_ref[:, o:o + MLA_NOPE] = k_nope[:, h * MLA_NOPE:(h + 1) * MLA_NOPE].astype(BF16)
        k_ref[:, o + MLA_NOPE:o + MLA_QKW] = kpe_b


def _mla_pre(proj, cos_t, sin_t, g_q, g_kv, w_uq, w_uk, w_uv_t):
    rows = proj.shape[0]
    tm = _tile(rows, 256, LANE)
    qkw = MLA_HEADS * MLA_QKW
    vw = MLA_HEADS * MLA_V
    row = lambda i: (i, 0)
    const = lambda i: (0, 0)
    return pl.pallas_call(
        _mla_pre_body,
        grid=(rows // tm,),
        in_specs=[
            pl.BlockSpec((tm, MLA_Q_LORA), lambda i: (i, AB_CQ // MLA_Q_LORA)),
            pl.BlockSpec((tm, MLA_KV_LORA), lambda i: (i, AB_CKV // MLA_KV_LORA)),
            pl.BlockSpec((tm, LANE), lambda i: (i, AB_KPE // LANE)),
            pl.BlockSpec((tm, LANE), row),
            pl.BlockSpec((tm, LANE), row),
            pl.BlockSpec((1, MLA_Q_LORA), const),
            pl.BlockSpec((1, MLA_KV_LORA), const),
            pl.BlockSpec((MLA_Q_LORA, qkw), const),
            pl.BlockSpec((MLA_KV_LORA, MLA_HEADS * MLA_NOPE), const),
            pl.BlockSpec((vw, MLA_KV_LORA), const),
        ],
        out_specs=[
            pl.BlockSpec((tm, qkw), row),
            pl.BlockSpec((tm, qkw), row),
            pl.BlockSpec((vw, tm), lambda i: (0, i)),
            pl.BlockSpec((tm, MLA_KV_LORA), row),
            pl.BlockSpec((tm, LANE), row),
        ],
        out_shape=[
            jax.ShapeDtypeStruct((rows, qkw), BF16),
            jax.ShapeDtypeStruct((rows, qkw), BF16),
            jax.ShapeDtypeStruct((vw, rows), BF16),
            jax.ShapeDtypeStruct((rows, MLA_KV_LORA), F32),
            jax.ShapeDtypeStruct((rows, LANE), F32),
        ],
        compiler_params=_params("parallel"),
        name="mla_pre",
    )(proj, proj, proj, cos_t, sin_t, g_q, g_kv, w_uq, w_uk, w_uv_t)


def _flash_body(q_ref, k_ref, vt_ref, o_ref, m_ref, l_ref, acc_ref, *, blk):
    qi = pl.program_id(1)
    q = q_ref[...]
    m_ref[...] = jnp.full(m_ref.shape, -jnp.inf, F32)
    l_ref[...] = jnp.zeros(l_ref.shape, F32)
    acc_ref[...] = jnp.zeros(acc_ref.shape, F32)

    def block(ki, masked):
        start = pl.multiple_of(ki * blk, blk)
        st = _dot_nt(k_ref[pl.ds(start, blk), :], q)
        if masked:
            k_chunk = lax.broadcasted_iota(jnp.int32, (blk, blk), 0) // CHUNK
            q_chunk = lax.broadcasted_iota(jnp.int32, (blk, blk), 1) // CHUNK
            st = jnp.where(k_chunk <= q_chunk, st, -jnp.inf)
        m_old = m_ref[...]
        m_new = jnp.maximum(m_old, jnp.max(st, axis=0, keepdims=True))
        alpha = jnp.exp2(m_old - m_new)
        p = jnp.exp2(st - m_new)
        l_ref[...] = alpha * l_ref[...] + jnp.sum(p, axis=0, keepdims=True)
        acc_ref[...] = alpha * acc_ref[...] + _dot(vt_ref[:, pl.ds(start, blk)], p.astype(BF16))
        m_ref[...] = m_new

    def interior(ki, carry):
        block(ki, False)
        return carry

    lax.fori_loop(0, qi, interior, 0)
    block(qi, True)
    o_ref[...] = (acc_ref[...] / l_ref[...]).T.astype(o_ref.dtype)


def _flash(q, k, vt, seq_len, tile):
    rows = q.shape[0]
    blk = _tile(seq_len, tile, LANE)
    assert blk % CHUNK == 0
    return pl.pallas_call(
        functools.partial(_flash_body, blk=blk),
        grid=(MLA_HEADS, seq_len // blk),
        in_specs=[
            pl.BlockSpec((blk, MLA_QKW), lambda h, qi: (qi, h)),
            pl.BlockSpec((seq_len, MLA_QKW), lambda h, qi: (0, h)),
            pl.BlockSpec((MLA_V, seq_len), lambda h, qi: (h, 0)),
        ],
        out_specs=pl.BlockSpec((blk, MLA_V), lambda h, qi: (qi, h)),
        out_shape=jax.ShapeDtypeStruct((rows, MLA_HEADS * MLA_V), BF16),
        scratch_shapes=[pltpu.VMEM((1, blk), F32), pltpu.VMEM((1, blk), F32), pltpu.VMEM((MLA_V, blk), F32)],
        compiler_params=_params("parallel", "arbitrary"),
        name="mla_flash",
    )(q, k, vt)


def _mla_hist_body(q_ref, kn_ref, ckvn_ref, ckv_ref, kpe_ref, wuk_ref, wuv_ref, dst_ref, o_ref, kpe_pad_ref, *, past_len):
    del dst_ref
    t_new = q_ref.shape[0]
    v_new = _dot(ckvn_ref[...].astype(BF16), wuv_ref[...]).astype(BF16)
    ckv_b = ckv_ref[0].astype(BF16)
    k_past = _dot(ckv_b, wuk_ref[...]).astype(BF16)
    v_past = _dot(ckv_b, wuv_ref[...]).astype(BF16)
    kpe_pad_ref[:, :MLA_ROPE] = kpe_ref[0].astype(BF16)
    kpe_pad_ref[:, MLA_ROPE:] = jnp.zeros((past_len, LANE - MLA_ROPE), BF16)
    kpe_past = kpe_pad_ref[...]

    q_chunk = (past_len + lax.broadcasted_iota(jnp.int32, (t_new, past_len), 0)) // CHUNK
    mask_past = lax.broadcasted_iota(jnp.int32, (t_new, past_len), 1) // CHUNK <= q_chunk
    q_chunk_n = (past_len + lax.broadcasted_iota(jnp.int32, (t_new, t_new), 0)) // CHUNK
    mask_new = (past_len + lax.broadcasted_iota(jnp.int32, (t_new, t_new), 1)) // CHUNK <= q_chunk_n

    for h in range(MLA_HEADS):
        o = h * MLA_QKW
        q_nope = q_ref[:, o:o + MLA_NOPE]
        q_pe = q_ref[:, o + MLA_NOPE:o + MLA_QKW]
        s_past = _dot_nt(q_nope, k_past[:, h * MLA_NOPE:(h + 1) * MLA_NOPE]) + _dot_nt(q_pe, kpe_past)
        s_new = _dot_nt(q_ref[:, o:o + MLA_QKW], kn_ref[:, o:o + MLA_QKW])
        s_past = jnp.where(mask_past, s_past, -jnp.inf)
        s_new = jnp.where(mask_new, s_new, -jnp.inf)
        m = jnp.maximum(jnp.max(s_past, axis=-1, keepdims=True), jnp.max(s_new, axis=-1, keepdims=True))
        p_past = jnp.exp2(s_past - m)
        p_new = jnp.exp2(s_new - m)
        denom = jnp.sum(p_past, axis=-1, keepdims=True) + jnp.sum(p_new, axis=-1, keepdims=True)
        pv = _dot(p_past.astype(BF16), v_past[:, h * MLA_V:(h + 1) * MLA_V])
        pv = pv + _dot(p_new.astype(BF16), v_new[:, h * MLA_V:(h + 1) * MLA_V])
        o_ref[:, h * MLA_V:(h + 1) * MLA_V] = (pv / denom).astype(o_ref.dtype)


def _mla_hist(q, k, ckv_n, row0, n_seq, t_new, ckv_past, kpe_past, w_uk, w_uv, dst):
    past_len = ckv_past.shape[1]
    base = row0 // t_new
    qkw = MLA_HEADS * MLA_QKW
    vw = MLA_HEADS * MLA_V
    row = lambda b: (base + b, 0)
    const = lambda b: (0, 0)
    return pl.pallas_call(
        functools.partial(_mla_hist_body, past_len=past_len),
        grid=(n_seq,),
        in_specs=[
            pl.BlockSpec((t_new, qkw), row),
            pl.BlockSpec((t_new, qkw), row),
            pl.BlockSpec((t_new, MLA_KV_LORA), row),
            pl.BlockSpec((1, past_len, MLA_KV_LORA), lambda b: (b, 0, 0)),
            pl.BlockSpec((1, past_len, MLA_ROPE), lambda b: (b, 0, 0)),
            pl.BlockSpec((MLA_KV_LORA, MLA_HEADS * MLA_NOPE), const),
            pl.BlockSpec((MLA_KV_LORA, vw), const),
            pl.BlockSpec(memory_space=pl.ANY),
        ],
        out_specs=pl.BlockSpec((t_new, vw), row),
        out_shape=jax.ShapeDtypeStruct(dst.shape, dst.dtype),
        scratch_shapes=[pltpu.VMEM((past_len, LANE), BF16)],
        input_output_aliases={7: 0},
        compiler_params=_params("parallel"),
        name="mla_hist",
    )(q, k, ckv_n, ckv_past, kpe_past, w_uk, w_uv, dst)


SSD_L = 128


def _conv_silu(x_ref, buf_ref, w_ref, b_ref, ext_ref, first):
    L = x_ref.shape[0]

    @pl.when(first)
    def _():
        ext_ref[8 - (SSD_CONV - 1):8, :] = buf_ref[0]

    @pl.when(jnp.logical_not(first))
    def _():
        ext_ref[0:8, :] = ext_ref[L:L + 8, :]

    ext_ref[8:8 + L, :] = x_ref[...]
    y = b_ref[...]
    for j in range(SSD_CONV):
        y = y + ext_ref[8 - (SSD_CONV - 1) + j:8 - (SSD_CONV - 1) + j + L, :] * w_ref[j:j + 1, :]
    return _silu(y)


def _pad_rows(x, n):
    if x.shape[0] == n:
        return x
    return jnp.concatenate([x, jnp.zeros((n - x.shape[0], x.shape[1]), x.dtype)], axis=0)


def _ssd_body(z_ref, x_ref, bm_ref, cm_ref, dt_ref, bufx_ref, bufb_ref, bufc_ref, wx_ref, wb_ref, wc_ref,
              bx_ref, bb_ref, bc_ref, dtb_ref, alog_ref, dsk_ref, gn_ref, s0_ref, *rest, aliased):
    y_ref, s_out_ref, s_ref, extx_ref, extb_ref, extc_ref = rest[1:] if aliased else rest
    c = pl.program_id(2)
    first = c == 0
    L = x_ref.shape[0]
    LS = SSD_L

    @pl.when(first)
    def _():
        s_ref[...] = s0_ref[0].reshape(SSD_GW, SSD_N).T

    xs = _conv_silu(x_ref, bufx_ref, wx_ref, bx_ref, extx_ref, first)
    bm = _conv_silu(bm_ref, bufb_ref, wb_ref, bb_ref, extb_ref, first)
    cm = _conv_silu(cm_ref, bufc_ref, wc_ref, bc_ref, extc_ref, first)

    dt = _softplus(dt_ref[...] + dtb_ref[0])
    xs_p = _pad_rows(xs, LS)
    bm_p = _pad_rows(bm, LS).astype(BF16)
    dt_p = _pad_rows(dt, LS)
    a_row = -jnp.exp(alog_ref[0])
    da = dt_p * a_row

    row = lax.broadcasted_iota(jnp.int32, (LS, LS), 0)
    col = lax.broadcasted_iota(jnp.int32, (LS, LS), 1)
    tri = jnp.where(row >= col, 1.0, 0.0).astype(BF16)
    cum = _dot_exact_rhs(tri, da)
    cum_t = cum.T
    dt_t = dt_p.T
    cum_q = cum[:L]
    last = cum[LS - 1:LS, :]

    erow = lax.broadcasted_iota(jnp.int32, (LANE, SSD_GW), 0)
    ecol = lax.broadcasted_iota(jnp.int32, (LANE, SSD_GW), 1) // SSD_HEADDIM
    expand = jnp.where(erow == ecol, 1.0, 0.0).astype(BF16)
    e_cum = _dot_exact_lhs(jnp.exp(cum_q), expand)
    w_dec = _dot_exact_lhs(jnp.exp(last - cum) * dt_p, expand)
    e_last = _dot_exact_lhs(jnp.exp(last), expand)

    cm_b = cm.astype(BF16)
    cb = _dot_nt(cm_b, bm_p)
    causal = lax.broadcasted_iota(jnp.int32, (L, LS), 0) >= lax.broadcasted_iota(jnp.int32, (L, LS), 1)
    lane_half = lax.broadcasted_iota(jnp.int32, (LS, LANE), 1) < SSD_HEADDIM

    def head_weights(r):
        diff = cum_q[:, r:r + 1] - cum_t[r:r + 1, :]
        return (cb * jnp.exp(jnp.where(causal, diff, -jnp.inf)) * dt_t[r:r + 1, :]).astype(BF16)

    y_parts = []
    for p in range(SSD_REP // 2):
        w_pair = jnp.concatenate([head_weights(2 * p), head_weights(2 * p + 1)], axis=1)
        x_pair = xs_p[:, p * LANE:(p + 1) * LANE]
        x_bd = jnp.concatenate([jnp.where(lane_half, x_pair, 0.0), jnp.where(lane_half, 0.0, x_pair)], axis=0)
        y_parts.append(_dot(w_pair, x_bd.astype(BF16)))
    y = jnp.concatenate(y_parts, axis=1)

    s_old = s_ref[...]
    y = y + _dot(cm_b, s_old.astype(BF16)) * e_cum
    s_new = s_old * e_last + _dot_tn(bm_p, (xs_p * w_dec).astype(BF16))
    s_ref[...] = s_new

    y = y + xs * dsk_ref[...]
    y = y * _silu(z_ref[...])
    y_ref[...] = _rms(y, gn_ref[...]).astype(y_ref.dtype)

    @pl.when(c == pl.num_programs(2) - 1)
    def _():
        s_out_ref[0] = s_new.T.reshape(SSD_REP, SSD_HEADDIM, SSD_N)


def _ssd(proj, row0, n_seq, seq_len, chunk, conv_buf, s0, w_conv, b_conv, dt_bias, a_log, d_skip, g_norm, dst=None):
    rows = proj.shape[0]
    d_inner = SSD_GROUPS * SSD_GW
    L = chunk
    n_chunks = seq_len // L
    base = row0 // L
    bn = SSD_GROUPS * SSD_N
    off_x, off_b, off_c, off_dt = d_inner, 2 * d_inner, 2 * d_inner + bn, 2 * d_inner + 2 * bn

    def rmap(col_blocks_off):
        return lambda b, g, c: (base + b * n_chunks + c, col_blocks_off + g)

    nb = SSD_CONV - 1
    in_specs = [
        pl.BlockSpec((L, SSD_GW), rmap(0)),
        pl.BlockSpec((L, SSD_GW), rmap(off_x // SSD_GW)),
        pl.BlockSpec((L, SSD_N), rmap(off_b // SSD_N)),
        pl.BlockSpec((L, SSD_N), rmap(off_c // SSD_N)),
        pl.BlockSpec((L, LANE), rmap(off_dt // LANE)),
        pl.BlockSpec((1, nb, SSD_GW), lambda b, g, c: (b, 0, g)),
        pl.BlockSpec((1, nb, SSD_N), lambda b, g, c: (b, 0, d_inner // SSD_N + g)),
        pl.BlockSpec((1, nb, SSD_N), lambda b, g, c: (b, 0, (d_inner + bn) // SSD_N + g)),
        pl.BlockSpec((SSD_CONV, SSD_GW), lambda b, g, c: (0, g)),
        pl.BlockSpec((SSD_CONV, SSD_N), lambda b, g, c: (0, d_inner // SSD_N + g)),
        pl.BlockSpec((SSD_CONV, SSD_N), lambda b, g, c: (0, (d_inner + bn) // SSD_N + g)),
        pl.BlockSpec((1, SSD_GW), lambda b, g, c: (0, g)),
        pl.BlockSpec((1, SSD_N), lambda b, g, c: (0, d_inner // SSD_N + g)),
        pl.BlockSpec((1, SSD_N), lambda b, g, c: (0, (d_inner + bn) // SSD_N + g)),
        pl.BlockSpec((1, 1, LANE), lambda b, g, c: (g, 0, 0)),
        pl.BlockSpec((1, 1, LANE), lambda b, g, c: (g, 0, 0)),
        pl.BlockSpec((1, SSD_GW), lambda b, g, c: (0, g)),
        pl.BlockSpec((1, SSD_GW), lambda b, g, c: (0, g)),
        pl.BlockSpec((1, SSD_REP, SSD_HEADDIM, SSD_N), lambda b, g, c: (b, g, 0, 0)),
    ]
    args = [proj] * 5 + [conv_buf] * 3 + [w_conv] * 3 + [b_conv] * 3 + [dt_bias, a_log, d_skip, g_norm, s0]
    aliases = {}
    if dst is not None:
        in_specs.append(pl.BlockSpec(memory_space=pl.ANY))
        args.append(dst)
        aliases = {len(args) - 1: 0}
    return pl.pallas_call(
        functools.partial(_ssd_body, aliased=dst is not None),
        grid=(n_seq, SSD_GROUPS, n_chunks),
        in_specs=in_specs,
        out_specs=[
            pl.BlockSpec((L, SSD_GW), lambda b, g, c: (base + b * n_chunks + c, g)),
            pl.BlockSpec((1, SSD_REP, SSD_HEADDIM, SSD_N), lambda b, g, c: (b, g, 0, 0)),
        ],
        out_shape=[
            jax.ShapeDtypeStruct((rows, d_inner), BF16),
            jax.ShapeDtypeStruct((n_seq, SSD_GROUPS * SSD_REP, SSD_HEADDIM, SSD_N), F32),
        ],
        scratch_shapes=[
            pltpu.VMEM((SSD_N, SSD_GW), F32),
            pltpu.VMEM((L + 8, SSD_GW), F32),
            pltpu.VMEM((L + 8, SSD_N), F32),
            pltpu.VMEM((L + 8, SSD_N), F32),
        ],
        input_output_aliases=aliases,
        compiler_params=_params("parallel", "parallel", "arbitrary"),
        name="ssd",
    )(*args)


def _rot_cols(w):
    half = w.shape[-1] // 2
    return jnp.concatenate([-w[..., half:], w[..., :half]], axis=-1)


def _prep_ab_in(w):
    d = w.shape[0]
    o = 0
    parts = {}
    for name, size in (("q", GLA_QK), ("k", GLA_QK), ("v", GLA_VW), ("gr", GLA_GATE_RANK), ("og", GLA_VW),
                       ("cq", MLA_Q_LORA), ("ckv", MLA_KV_LORA), ("kpe", MLA_ROPE)):
        parts[name] = w[:, o:o + size]
        o += size
    zeros = jnp.zeros((d, LANE - GLA_GATE_RANK), w.dtype)
    out = jnp.concatenate([parts["q"], parts["k"], parts["v"], parts["og"], parts["cq"], parts["ckv"],
                           parts["kpe"], _rot_cols(parts["kpe"]), parts["gr"], zeros], axis=1)
    assert out.shape[1] == AB_W
    return out.astype(BF16)


def _prep_uq(w):
    w = w.reshape(w.shape[0], MLA_HEADS, MLA_NOPE + MLA_ROPE)
    pe = w[..., MLA_NOPE:]
    out = jnp.concatenate([w[..., :MLA_NOPE], pe, _rot_cols(pe)], axis=-1)
    return out.reshape(w.shape[0], MLA_HEADS * MLA_QKW).astype(BF16)


def _prep_ssd_in(w, d_inner):
    d = w.shape[0]
    n_heads = SSD_GROUPS * SSD_REP
    main = w[:, :2 * d_inner + 2 * SSD_GROUPS * SSD_N]
    dt = w[:, 2 * d_inner + 2 * SSD_GROUPS * SSD_N:].reshape(d, SSD_GROUPS, SSD_REP)
    assert dt.shape[1] * dt.shape[2] == n_heads
    dt = jnp.concatenate([dt, jnp.zeros((d, SSD_GROUPS, LANE - SSD_REP), w.dtype)], axis=-1)
    return jnp.concatenate([main, dt.reshape(d, SSD_GROUPS * LANE)], axis=1).astype(BF16)


def _head_table(v):
    v = v.reshape(SSD_GROUPS, 1, SSD_REP)
    return jnp.concatenate([v, jnp.zeros((SSD_GROUPS, 1, LANE - SSD_REP), v.dtype)], axis=-1)


def _rope_tables(pos):
    half = MLA_ROPE // 2
    inv = ROPE_THETA ** (-jnp.arange(half, dtype=F32) / half)
    ang = pos.astype(F32)[:, None] * inv[None, :]
    zeros = jnp.zeros((pos.shape[0], LANE - MLA_ROPE), F32)
    cos = jnp.cos(ang)
    sin = jnp.sin(ang)
    return jnp.concatenate([cos, cos, zeros], axis=1), jnp.concatenate([sin, sin, zeros], axis=1)


def kernel(x_prompt, x_sample, cache_mla_ckv, cache_mla_kpe, state_gla, state_ssd, state_ssd_conv, g_ffn1, w_ffn1_gate, w_ffn1_up, w_ffn1_down, g_mix, g_ffn2, w_ffn2_gate, w_ffn2_up, w_ffn2_down, w_ab_in, w_gla_gate_up, b_gla_gate, g_gla_norm, g_mla_q_norm, w_mla_uq, g_mla_kv_norm, w_mla_uk, w_mla_uv, w_ab_out, w_ssd_in, w_ssd_conv, b_ssd_conv, ssd_dt_bias, ssd_a_log, ssd_d, g_ssd_norm, w_ssd_out, g_final):
    n_p, t_p, d = x_prompt.shape
    n_s, t_s, _ = x_sample.shape
    assert n_p == 1
    depth = g_ffn1.shape[0]
    past_len = cache_mla_ckv.shape[2]
    rows_p = n_p * t_p
    d_inner = w_ssd_out.shape[1]
    assert d_inner == SSD_GROUPS * SSD_GW

    x = jnp.concatenate([x_prompt.reshape(rows_p, d), x_sample.reshape(n_s * t_s, d)], axis=0)
    pos = jnp.concatenate([jnp.arange(t_p, dtype=jnp.int32),
                           jnp.tile(past_len + jnp.arange(t_s, dtype=jnp.int32), n_s)])
    cos_t, sin_t = _rope_tables(pos)
    row2 = lambda v: v.reshape(1, -1)

    outs = {k: [] for k in ("p_ckv", "p_kpe", "p_gla", "p_ssd", "p_conv", "s_ckv", "s_kpe", "s_gla", "s_ssd", "s_conv")}
    for layer in range(depth):
        i = layer // 2
        x = _ffn(x, row2(g_ffn1[layer]), w_ffn1_gate[layer].astype(BF16), w_ffn1_up[layer].astype(BF16),
                 w_ffn1_down[layer].astype(BF16))
        if layer % 2 == 0:
            proj = _norm_proj(x, row2(g_mix[layer]), _prep_ab_in(w_ab_in[i]), AB_W // 2)
            w_gate = jnp.concatenate([w_gla_gate_up[i], jnp.zeros((LANE - GLA_GATE_RANK, GLA_QK), F32)], axis=0).astype(BF16)
            gla_args = (w_gate, row2(b_gla_gate[i]), row2(g_gla_norm[i]))
            gla_chunk = min(CHUNK, t_p)
            o_gla, p_gla = _gla(proj, 0, n_p, t_p, gla_chunk, min(4, t_p // gla_chunk),
                                jnp.zeros((n_p, GLA_HEADS, GLA_DK, GLA_DV), F32), *gla_args)
            o_gla, s_gla = _gla(proj, rows_p, n_s, t_s, t_s, 1, state_gla[i], *gla_args, dst=o_gla)

            w_uk = w_mla_uk[i].astype(BF16)
            w_uv = w_mla_uv[i].astype(BF16)
            q, k, vt, ckv_n, kpe_r = _mla_pre(proj, cos_t, sin_t, row2(g_mla_q_norm[i]), row2(g_mla_kv_norm[i]),
                                              _prep_uq(w_mla_uq[i]), w_uk, w_uv.T)
            o_mla = _flash(q, k, vt, t_p, 512)
            o_mla = _mla_hist(q, k, ckv_n, rows_p, n_s, t_s, cache_mla_ckv[i], cache_mla_kpe[i], w_uk, w_uv, o_mla)

            w_out = w_ab_out[i].astype(BF16)
            x = _out_proj(x, [o_gla, o_mla], [w_out[:GLA_VW], w_out[GLA_VW:]])
            outs["p_ckv"].append(ckv_n[:rows_p].reshape(n_p, t_p, MLA_KV_LORA))
            outs["s_ckv"].append(ckv_n[rows_p:].reshape(n_s, t_s, MLA_KV_LORA))
            outs["p_kpe"].append(kpe_r[:rows_p, :MLA_ROPE].reshape(n_p, t_p, MLA_ROPE))
            outs["s_kpe"].append(kpe_r[rows_p:, :MLA_ROPE].reshape(n_s, t_s, MLA_ROPE))
            outs["p_gla"].append(p_gla)
            outs["s_gla"].append(s_gla)
        else:
            n_heads = SSD_GROUPS * SSD_REP
            conv_ch = d_inner + 2 * SSD_GROUPS * SSD_N
            proj = _norm_proj(x, row2(g_mix[layer]), _prep_ssd_in(w_ssd_in[i], d_inner),
                              (2 * d_inner + 2 * SSD_GROUPS * SSD_N + SSD_GROUPS * LANE) // 8)
            ssd_args = (w_ssd_conv[i], row2(b_ssd_conv[i]), _head_table(ssd_dt_bias[i]), _head_table(ssd_a_log[i]),
                        row2(jnp.repeat(ssd_d[i], SSD_HEADDIM)), row2(g_ssd_norm[i]))
            y, p_ssd = _ssd(proj, 0, n_p, t_p, min(SSD_L, t_p), jnp.zeros((n_p, SSD_CONV - 1, conv_ch), F32),
                            jnp.zeros((n_p, n_heads, SSD_HEADDIM, SSD_N), F32), *ssd_args)
            y, s_ssd = _ssd(proj, rows_p, n_s, t_s, t_s, state_ssd_conv[i], state_ssd[i], *ssd_args, dst=y)
            x = _out_proj(x, [y], [w_ssd_out[i].astype(BF16)])
            xbc = proj[:, d_inner:d_inner + conv_ch]
            outs["p_conv"].append(xbc[:rows_p].reshape(n_p, t_p, conv_ch)[:, t_p - (SSD_CONV - 1):])
            outs["s_conv"].append(xbc[rows_p:].reshape(n_s, t_s, conv_ch)[:, t_s - (SSD_CONV - 1):])
            outs["p_ssd"].append(p_ssd)
            outs["s_ssd"].append(s_ssd)
        x = _ffn(x, row2(g_ffn2[layer]), w_ffn2_gate[layer].astype(BF16), w_ffn2_up[layer].astype(BF16),
                 w_ffn2_down[layer].astype(BF16), g_final=row2(g_final) if layer == depth - 1 else None)

    y_prompt = x[:rows_p].reshape(n_p, t_p, d)
    y_sample = x[rows_p:].reshape(n_s, t_s, d)
    st = {k: jnp.stack(v) for k, v in outs.items()}
    return (y_prompt, y_sample, st["p_ckv"], st["p_kpe"], st["p_gla"], st["p_ssd"], st["p_conv"],
            st["s_ckv"], st["s_kpe"], st["s_gla"], st["s_ssd"], st["s_conv"])
```

```python
import functools

import jax
import jax.numpy as jnp
from jax import lax
from jax.experimental import pallas as pl
from jax.experimental.pallas import tpu as pltpu

F32 = jnp.float32
BF16 = jnp.bfloat16

RMS_EPS = 1e-6
CHUNK = 64

GLA_HEADS = 4
GLA_DK = 128
GLA_DV = 256
GLA_GATE_RANK = 16
GLA_TAU = 16.0
GLA_QK = GLA_HEADS * GLA_DK
GLA_VW = GLA_HEADS * GLA_DV

MLA_HEADS = 8
MLA_Q_LORA = 512
MLA_KV_LORA = 512
MLA_NOPE = 128
MLA_ROPE = 64
MLA_V = 128
ROPE_THETA = 10000.0
MLA_QKW = 256
MLA_SCORE_SCALE = (MLA_NOPE + MLA_ROPE) ** -0.5 * 1.4426950408889634

SSD_HEADDIM = 64
SSD_GROUPS = 8
SSD_N = 128
SSD_CONV = 4
SSD_GW = 512
SSD_REP = SSD_GW // SSD_HEADDIM

LANE = 128
AB_Q, AB_K, AB_V, AB_OG, AB_CQ, AB_CKV, AB_KPE, AB_GR, AB_W = 0, 512, 1024, 2048, 3072, 3584, 4096, 4224, 4352

VMEM_LIMIT = 56 * 1024 * 1024


def _tile(n, target, align):
    best = None
    for t in range(align, min(n, target) + 1, align):
        if n % t == 0:
            best = t
    return best if best is not None else n


def _rms(x, g):
    return x * lax.rsqrt(jnp.mean(x * x, axis=-1, keepdims=True) + RMS_EPS) * g


def _silu(x):
    return x * jax.nn.sigmoid(x)


def _softplus(x):
    return jnp.maximum(x, 0.0) + jnp.log1p(jnp.exp(-jnp.abs(x)))


def _dot(a, b):
    return jnp.dot(a, b, preferred_element_type=F32)


def _dot_nt(a, b):
    return lax.dot_general(a, b, (((1,), (1,)), ((), ())), preferred_element_type=F32)


def _dot_tn(a, b):
    return lax.dot_general(a, b, (((0,), (0,)), ((), ())), preferred_element_type=F32)


def _split3(x):
    h = x.astype(BF16)
    r = x - h.astype(F32)
    m = r.astype(BF16)
    l = (r - m.astype(F32)).astype(BF16)
    return h, m, l


def _dot_exact_rhs(a_bf16, x):
    h, m, l = _split3(x)
    return _dot(a_bf16, h) + _dot(a_bf16, m) + _dot(a_bf16, l)


def _dot_exact_lhs(x, b_bf16):
    h, m, l = _split3(x)
    return _dot(h, b_bf16) + _dot(m, b_bf16) + _dot(l, b_bf16)


def _params(*sem):
    return pltpu.CompilerParams(dimension_semantics=sem, vmem_limit_bytes=VMEM_LIMIT)


def _ffn_body(x_ref, g_ref, wg_ref, wu_ref, wd_ref, gf_ref, o_ref, h_ref, *, final_norm):
    j = pl.program_id(1)

    @pl.when(j == 0)
    def _():
        x = x_ref[...]
        h_ref[...] = _rms(x, g_ref[...]).astype(BF16)
        o_ref[...] = x

    h = h_ref[...]
    a = _dot(h, wg_ref[...].astype(BF16))
    b = _dot(h, wu_ref[...].astype(BF16))
    act = (_silu(a) * b) * 0.5
    o_ref[...] += _dot(act.astype(BF16), wd_ref[...].astype(BF16))

    if final_norm:
        @pl.when(j == pl.num_programs(1) - 1)
        def _():
            o_ref[...] = _rms(o_ref[...], gf_ref[...])


def _ffn(x, g, wg, wu, wd, layer, g_final=None):
    rows, d = x.shape
    f = wg.shape[2]
    tm = _tile(rows, 1056, 8)
    tf = _tile(f, 256, LANE)
    final = g_final is not None
    return pl.pallas_call(
        functools.partial(_ffn_body, final_norm=final),
        grid=(rows // tm, f // tf),
        in_specs=[
            pl.BlockSpec((tm, d), lambda i, j: (i, 0), pipeline_mode=pl.Buffered(1)),
            pl.BlockSpec((1, d), lambda i, j: (0, 0)),
            pl.BlockSpec((None, d, tf), lambda i, j: (layer, 0, j)),
            pl.BlockSpec((None, d, tf), lambda i, j: (layer, 0, j)),
            pl.BlockSpec((None, tf, d), lambda i, j: (layer, j, 0)),
            pl.BlockSpec((1, d), lambda i, j: (0, 0)),
        ],
        out_specs=pl.BlockSpec((tm, d), lambda i, j: (i, 0)),
        out_shape=jax.ShapeDtypeStruct((rows, d), F32),
        scratch_shapes=[pltpu.VMEM((tm, d), BF16)],
        compiler_params=_params("parallel", "arbitrary"),
        name="ffn",
    )(x, g, wg, wu, wd, g_final if final else g)


def _norm_proj_body(x_ref, g_ref, w_ref, o_ref, h_ref):
    @pl.when(pl.program_id(1) == 0)
    def _():
        h_ref[...] = _rms(x_ref[...], g_ref[...]).astype(BF16)

    o_ref[...] = _dot(h_ref[...], w_ref[...])


def _norm_proj(x, g, w, tn):
    rows, d = x.shape
    n = w.shape[1]
    tm = _tile(rows, 768, 8)
    return pl.pallas_call(
        _norm_proj_body,
        grid=(rows // tm, n // tn),
        in_specs=[
            pl.BlockSpec((tm, d), lambda i, j: (i, 0)),
            pl.BlockSpec((1, d), lambda i, j: (0, 0)),
            pl.BlockSpec((d, tn), lambda i, j: (0, j)),
        ],
        out_specs=pl.BlockSpec((tm, tn), lambda i, j: (i, j)),
        out_shape=jax.ShapeDtypeStruct((rows, n), F32),
        scratch_shapes=[pltpu.VMEM((tm, d), BF16)],
        compiler_params=_params("parallel", "arbitrary"),
        name="norm_proj",
    )(x, g, w)


def _out_proj_body(*refs, n_in):
    x_ref = refs[0]
    o_ref = refs[1 + 2 * n_in]
    acc = x_ref[...]
    for i in range(n_in):
        acc = acc + _dot(refs[1 + i][...], refs[1 + n_in + i][...])
    o_ref[...] = acc


def _out_proj(x, ys, ws):
    rows, d = x.shape
    n_in = len(ys)
    tm = _tile(rows, 768, 8)
    tn = _tile(d, 1024, LANE)
    in_specs = [pl.BlockSpec((tm, tn), lambda i, j: (i, j))]
    in_specs += [pl.BlockSpec((tm, y.shape[1]), lambda i, j: (i, 0)) for y in ys]
    in_specs += [pl.BlockSpec((w.shape[0], tn), lambda i, j: (0, j)) for w in ws]
    return pl.pallas_call(
        functools.partial(_out_proj_body, n_in=n_in),
        grid=(rows // tm, d // tn),
        in_specs=in_specs,
        out_specs=pl.BlockSpec((tm, tn), lambda i, j: (i, j)),
        out_shape=jax.ShapeDtypeStruct((rows, d), F32),
        compiler_params=_params("parallel", "arbitrary"),
        name="out_proj",
    )(x, *ys, *ws)


def _gla_body(q_ref, k_ref, v_ref, og_ref, gr_ref, s0_ref, wg_ref, bg_ref, gn_ref, *rest, aliased, L):
    o_ref, s_out_ref, s_ref = rest[1:] if aliased else rest
    step = pl.program_id(1)
    n_sub = q_ref.shape[0] // L

    @pl.when(step == 0)
    def _():
        for h in range(GLA_HEADS):
            s_ref[h] = s0_ref[0, h].T

    pre = _dot(gr_ref[...].astype(BF16), wg_ref[...]) + bg_ref[...]
    log_a = (jnp.minimum(pre, 0.0) - jnp.log1p(jnp.exp(-jnp.abs(pre)))) * (1.0 / GLA_TAU)
    causal = lax.broadcasted_iota(jnp.int32, (L, L), 0) >= lax.broadcasted_iota(jnp.int32, (L, L), 1)
    tri = jnp.where(causal, 1.0, 0.0).astype(BF16)
    gn = gn_ref[...]

    states = [s_ref[h] for h in range(GLA_HEADS)]
    for c in range(n_sub):
        r0 = c * L
        b_all = _dot_exact_rhs(tri, log_a[r0:r0 + L, :])
        for h in range(GLA_HEADS):
            kc = slice(h * GLA_DK, (h + 1) * GLA_DK)
            vc = slice(h * GLA_DV, (h + 1) * GLA_DV)
            b = b_all[:, kc]
            b_mid = b[L // 2 - 1:L // 2, :]
            b_last = b[L - 1:L, :]
            q = q_ref[r0:r0 + L, kc] * (GLA_DK ** -0.5)
            k = k_ref[r0:r0 + L, kc]
            v = v_ref[r0:r0 + L, vc].astype(BF16)
            q_in = (q * jnp.exp(b)).astype(BF16)
            q_mid = (q * jnp.exp(b - b_mid)).astype(BF16)
            k_mid = (k * jnp.exp(b_mid - b)).astype(BF16)
            k_end = (k * jnp.exp(b_last - b)).astype(BF16)
            s_old = states[h]
            att = jnp.where(causal, _dot_nt(q_mid, k_mid), 0.0)
            o = _dot(att.astype(BF16), v) + _dot_nt(q_in, s_old.astype(BF16))
            states[h] = s_old * jnp.exp(b_last) + _dot_tn(v, k_end)
            o_ref[r0:r0 + L, vc] = (_rms(o, gn) * _silu(og_ref[r0:r0 + L, vc])).astype(o_ref.dtype)

    for h in range(GLA_HEADS):
        s_ref[h] = states[h]

    @pl.when(step == pl.num_programs(1) - 1)
    def _():
        for h in range(GLA_HEADS):
            s_out_ref[0, h] = states[h].T


def _gla(proj, row0, n_seq, seq_len, chunk, n_sub, s0, w_gate, b_gate, g_norm, dst=None):
    rows = proj.shape[0]
    tr = chunk * n_sub
    n_steps = seq_len // tr
    base = row0 // tr

    def rmap(col_block):
        return lambda b, s: (base + b * n_steps + s, col_block)

    const = lambda b, s: (0, 0)
    in_specs = [
        pl.BlockSpec((tr, GLA_QK), rmap(AB_Q // GLA_QK)),
        pl.BlockSpec((tr, GLA_QK), rmap(AB_K // GLA_QK)),
        pl.BlockSpec((tr, GLA_VW), rmap(AB_V // GLA_VW)),
        pl.BlockSpec((tr, GLA_VW), rmap(AB_OG // GLA_VW)),
        pl.BlockSpec((tr, LANE), rmap(AB_GR // LANE)),
        pl.BlockSpec((1, GLA_HEADS, GLA_DK, GLA_DV), lambda b, s: (b, 0, 0, 0)),
        pl.BlockSpec((LANE, GLA_QK), const),
        pl.BlockSpec((1, GLA_QK), const),
        pl.BlockSpec((1, GLA_DV), const),
    ]
    args = [proj, proj, proj, proj, proj, s0, w_gate, b_gate, g_norm]
    aliases = {}
    if dst is not None:
        in_specs.append(pl.BlockSpec(memory_space=pl.ANY))
        args.append(dst)
        aliases = {len(args) - 1: 0}
    return pl.pallas_call(
        functools.partial(_gla_body, aliased=dst is not None, L=chunk),
        grid=(n_seq, n_steps),
        in_specs=in_specs,
        out_specs=[
            pl.BlockSpec((tr, GLA_VW), rmap(0)),
            pl.BlockSpec((1, GLA_HEADS, GLA_DK, GLA_DV), lambda b, s: (b, 0, 0, 0)),
        ],
        out_shape=[
            jax.ShapeDtypeStruct((rows, GLA_VW), BF16),
            jax.ShapeDtypeStruct((n_seq, GLA_HEADS, GLA_DK, GLA_DV), F32),
        ],
        scratch_shapes=[pltpu.VMEM((GLA_HEADS, GLA_DV, GLA_DK), F32)],
        input_output_aliases=aliases,
        compiler_params=_params("parallel", "arbitrary"),
        name="gla",
    )(*args)


def _rope128(x, cos_t, sin_t):
    return x * cos_t + pltpu.roll(x, MLA_ROPE, 1) * sin_t


def _mla_pre_body(cq_ref, ckv_ref, kpe_ref, cos_ref, sin_ref, gq_ref, gkv_ref, wuq_ref, wuk_ref, wuvt_ref,
                  q_ref, k_ref, vt_ref, ckvn_ref, kper_ref):
    scale = MLA_SCORE_SCALE
    cos_t = cos_ref[...]
    sin_t = sin_ref[...]
    qf = _dot(_rms(cq_ref[...], gq_ref[...]).astype(BF16), wuq_ref[...])
    ckv_n = _rms(ckv_ref[...], gkv_ref[...])
    ckvn_ref[...] = ckv_n
    kpe_r = _rope128(kpe_ref[...], cos_t, sin_t)
    kper_ref[...] = kpe_r
    kpe_b = kpe_r.astype(BF16)
    ckv_b = ckv_n.astype(BF16)
    k_nope = _dot(ckv_b, wuk_ref[...])
    vt_ref[...] = _dot_nt(wuvt_ref[...], ckv_b).astype(BF16)
    for h in range(MLA_HEADS):
        o = h * MLA_QKW
        q_ref[:, o:o + MLA_NOPE] = (qf[:, o:o + MLA_NOPE] * scale).astype(BF16)
        q_ref[:, o + MLA_NOPE:o + MLA_QKW] = (_rope128(qf[:, o + MLA_NOPE:o + MLA_QKW], cos_t, sin_t) * scale).astype(BF16)
        k_ref[:, o:o + MLA_NOPE] = k_nope[:, h * MLA_NOPE:(h + 1) * MLA_NOPE].astype(BF16)
        k_ref[:, o + MLA_NOPE:o + MLA_QKW] = kpe_b


def _mla_pre(proj, cos_t, sin_t, g_q, g_kv, w_uq, w_uk, w_uv_t):
    rows = proj.shape[0]
    tm = _tile(rows, 256, LANE)
    qkw = MLA_HEADS * MLA_QKW
    vw = MLA_HEADS * MLA_V
    row = lambda i: (i, 0)
    const = lambda i: (0, 0)
    return pl.pallas_call(
        _mla_pre_body,
        grid=(rows // tm,),
        in_specs=[
            pl.BlockSpec((tm, MLA_Q_LORA), lambda i: (i, AB_CQ // MLA_Q_LORA)),
            pl.BlockSpec((tm, MLA_KV_LORA), lambda i: (i, AB_CKV // MLA_KV_LORA)),
            pl.BlockSpec((tm, LANE), lambda i: (i, AB_KPE // LANE)),
            pl.BlockSpec((tm, LANE), row),
            pl.BlockSpec((tm, LANE), row),
            pl.BlockSpec((1, MLA_Q_LORA), const),
            pl.BlockSpec((1, MLA_KV_LORA), const),
            pl.BlockSpec((MLA_Q_LORA, qkw), const),
            pl.BlockSpec((MLA_KV_LORA, MLA_HEADS * MLA_NOPE), const),
            pl.BlockSpec((vw, MLA_KV_LORA), const),
        ],
        out_specs=[
            pl.BlockSpec((tm, qkw), row),
            pl.BlockSpec((tm, qkw), row),
            pl.BlockSpec((vw, tm), lambda i: (0, i)),
            pl.BlockSpec((tm, MLA_KV_LORA), row),
            pl.BlockSpec((tm, LANE), row),
        ],
        out_shape=[
            jax.ShapeDtypeStruct((rows, qkw), BF16),
            jax.ShapeDtypeStruct((rows, qkw), BF16),
            jax.ShapeDtypeStruct((vw, rows), BF16),
            jax.ShapeDtypeStruct((rows, MLA_KV_LORA), F32),
            jax.ShapeDtypeStruct((rows, LANE), F32),
        ],
        compiler_params=_params("parallel"),
        name="mla_pre",
    )(proj, proj, proj, cos_t, sin_t, g_q, g_kv, w_uq, w_uk, w_uv_t)


def _flash_body(q_ref, k_ref, vt_ref, dst_ref, o_ref, m_ref, l_ref, acc_ref, *, blk):
    del dst_ref
    qi = pl.program_id(1)
    q = q_ref[...]
    m_ref[...] = jnp.full(m_ref.shape, -jnp.inf, F32)
    l_ref[...] = jnp.zeros(l_ref.shape, F32)
    acc_ref[...] = jnp.zeros(acc_ref.shape, F32)

    def block(ki, masked):
        start = pl.multiple_of(ki * blk, blk)
        st = _dot_nt(k_ref[pl.ds(start, blk), :], q)
        if masked:
            k_chunk = lax.broadcasted_iota(jnp.int32, (blk, blk), 0) // CHUNK
            q_chunk = lax.broadcasted_iota(jnp.int32, (blk, blk), 1) // CHUNK
            st = jnp.where(k_chunk <= q_chunk, st, -jnp.inf)
        m_old = m_ref[...]
        m_new = jnp.maximum(m_old, jnp.max(st, axis=0, keepdims=True))
        alpha = jnp.exp2(m_old - m_new)
        p = jnp.exp2(st - m_new)
        l_ref[...] = alpha * l_ref[...] + jnp.sum(p, axis=0, keepdims=True)
        acc_ref[...] = alpha * acc_ref[...] + _dot(vt_ref[:, pl.ds(start, blk)], p.astype(BF16))
        m_ref[...] = m_new

    def interior(ki, carry):
        block(ki, False)
        return carry

    lax.fori_loop(0, qi, interior, 0)
    block(qi, True)
    o_ref[...] = (acc_ref[...] / l_ref[...]).T.astype(o_ref.dtype)


def _flash(q, k, vt, seq_len, tile, dst):
    blk = _tile(seq_len, tile, LANE)
    assert blk % CHUNK == 0
    return pl.pallas_call(
        functools.partial(_flash_body, blk=blk),
        grid=(MLA_HEADS, seq_len // blk),
        in_specs=[
            pl.BlockSpec((blk, MLA_QKW), lambda h, qi: (qi, h)),
            pl.BlockSpec((seq_len, MLA_QKW), lambda h, qi: (0, h)),
            pl.BlockSpec((MLA_V, seq_len), lambda h, qi: (h, 0)),
            pl.BlockSpec(memory_space=pl.ANY),
        ],
        out_specs=pl.BlockSpec((blk, MLA_V), lambda h, qi: (qi, h)),
        out_shape=jax.ShapeDtypeStruct(dst.shape, dst.dtype),
        scratch_shapes=[pltpu.VMEM((1, blk), F32), pltpu.VMEM((1, blk), F32), pltpu.VMEM((MLA_V, blk), F32)],
        input_output_aliases={3: 0},
        compiler_params=_params("parallel", "arbitrary"),
        name="mla_flash",
    )(q, k, vt, dst)


def _mla_hist_body(q_ref, kn_ref, ckvn_ref, ckv_ref, kpe_ref, wuk_ref, wuv_ref, dst_ref, o_ref, kpe_pad_ref, *, past_len):
    del dst_ref
    t_new = q_ref.shape[0]
    v_new = _dot(ckvn_ref[...].astype(BF16), wuv_ref[...]).astype(BF16)
    ckv_b = ckv_ref[0].astype(BF16)
    k_past = _dot(ckv_b, wuk_ref[...]).astype(BF16)
    v_past = _dot(ckv_b, wuv_ref[...]).astype(BF16)
    kpe_pad_ref[:, :MLA_ROPE] = kpe_ref[0].astype(BF16)
    kpe_pad_ref[:, MLA_ROPE:] = jnp.zeros((past_len, LANE - MLA_ROPE), BF16)
    kpe_past = kpe_pad_ref[...]

    q_chunk = (past_len + lax.broadcasted_iota(jnp.int32, (t_new, past_len), 0)) // CHUNK
    mask_past = lax.broadcasted_iota(jnp.int32, (t_new, past_len), 1) // CHUNK <= q_chunk
    q_chunk_n = (past_len + lax.broadcasted_iota(jnp.int32, (t_new, t_new), 0)) // CHUNK
    mask_new = (past_len + lax.broadcasted_iota(jnp.int32, (t_new, t_new), 1)) // CHUNK <= q_chunk_n

    for h in range(MLA_HEADS):
        o = h * MLA_QKW
        q_nope = q_ref[:, o:o + MLA_NOPE]
        q_pe = q_ref[:, o + MLA_NOPE:o + MLA_QKW]
        s_past = _dot_nt(q_nope, k_past[:, h * MLA_NOPE:(h + 1) * MLA_NOPE]) + _dot_nt(q_pe, kpe_past)
        s_new = _dot_nt(q_ref[:, o:o + MLA_QKW], kn_ref[:, o:o + MLA_QKW])
        s_past = jnp.where(mask_past, s_past, -jnp.inf)
        s_new = jnp.where(mask_new, s_new, -jnp.inf)
        m = jnp.maximum(jnp.max(s_past, axis=-1, keepdims=True), jnp.max(s_new, axis=-1, keepdims=True))
        p_past = jnp.exp2(s_past - m)
        p_new = jnp.exp2(s_new - m)
        denom = jnp.sum(p_past, axis=-1, keepdims=True) + jnp.sum(p_new, axis=-1, keepdims=True)
        pv = _dot(p_past.astype(BF16), v_past[:, h * MLA_V:(h + 1) * MLA_V])
        pv = pv + _dot(p_new.astype(BF16), v_new[:, h * MLA_V:(h + 1) * MLA_V])
        o_ref[:, h * MLA_V:(h + 1) * MLA_V] = (pv / denom).astype(o_ref.dtype)


def _mla_hist(q, k, ckv_n, row0, n_seq, t_new, ckv_past, kpe_past, w_uk, w_uv, dst):
    past_len = ckv_past.shape[1]
    base = row0 // t_new
    qkw = MLA_HEADS * MLA_QKW
    vw = MLA_HEADS * MLA_V
    row = lambda b: (base + b, 0)
    const = lambda b: (0, 0)
    return pl.pallas_call(
        functools.partial(_mla_hist_body, past_len=past_len),
        grid=(n_seq,),
        in_specs=[
            pl.BlockSpec((t_new, qkw), row),
            pl.BlockSpec((t_new, qkw), row),
            pl.BlockSpec((t_new, MLA_KV_LORA), row),
            pl.BlockSpec((1, past_len, MLA_KV_LORA), lambda b: (b, 0, 0)),
            pl.BlockSpec((1, past_len, MLA_ROPE), lambda b: (b, 0, 0)),
            pl.BlockSpec((MLA_KV_LORA, MLA_HEADS * MLA_NOPE), const),
            pl.BlockSpec((MLA_KV_LORA, vw), const),
            pl.BlockSpec(memory_space=pl.ANY),
        ],
        out_specs=pl.BlockSpec((t_new, vw), row),
        out_shape=jax.ShapeDtypeStruct(dst.shape, dst.dtype),
        scratch_shapes=[pltpu.VMEM((past_len, LANE), BF16)],
        input_output_aliases={7: 0},
        compiler_params=_params("parallel"),
        name="mla_hist",
    )(q, k, ckv_n, ckv_past, kpe_past, w_uk, w_uv, dst)


SSD_L = 128


def _conv_silu(x_ref, buf_ref, w_ref, b_ref, ext_ref, first):
    L = x_ref.shape[0]

    @pl.when(first)
    def _():
        ext_ref[8 - (SSD_CONV - 1):8, :] = buf_ref[0]

    @pl.when(jnp.logical_not(first))
    def _():
        ext_ref[0:8, :] = ext_ref[L:L + 8, :]

    ext_ref[8:8 + L, :] = x_ref[...]
    y = b_ref[...]
    for j in range(SSD_CONV):
        y = y + ext_ref[8 - (SSD_CONV - 1) + j:8 - (SSD_CONV - 1) + j + L, :] * w_ref[j:j + 1, :]
    return _silu(y)


def _pad_rows(x, n):
    if x.shape[0] == n:
        return x
    return jnp.concatenate([x, jnp.zeros((n - x.shape[0], x.shape[1]), x.dtype)], axis=0)


def _ssd_body(z_ref, x_ref, bm_ref, cm_ref, dt_ref, bufx_ref, bufb_ref, bufc_ref, wx_ref, wb_ref, wc_ref,
              bx_ref, bb_ref, bc_ref, dtb_ref, alog_ref, dsk_ref, gn_ref, s0_ref, *rest, aliased):
    y_ref, s_out_ref, s_ref, extx_ref, extb_ref, extc_ref = rest[1:] if aliased else rest
    c = pl.program_id(2)
    first = c == 0
    L = x_ref.shape[0]
    LS = SSD_L

    @pl.when(first)
    def _():
        s_ref[...] = s0_ref[0].reshape(SSD_GW, SSD_N).T

    xs = _conv_silu(x_ref, bufx_ref, wx_ref, bx_ref, extx_ref, first)
    bm = _conv_silu(bm_ref, bufb_ref, wb_ref, bb_ref, extb_ref, first)
    cm = _conv_silu(cm_ref, bufc_ref, wc_ref, bc_ref, extc_ref, first)

    dt = _softplus(dt_ref[...] + dtb_ref[0])
    xs_p = _pad_rows(xs, LS)
    bm_p = _pad_rows(bm, LS).astype(BF16)
    dt_p = _pad_rows(dt, LS)
    a_row = -jnp.exp(alog_ref[0])
    da = dt_p * a_row

    row = lax.broadcasted_iota(jnp.int32, (LS, LS), 0)
    col = lax.broadcasted_iota(jnp.int32, (LS, LS), 1)
    tri = jnp.where(row >= col, 1.0, 0.0).astype(BF16)
    cum = _dot_exact_rhs(tri, da)
    cum_t = cum.T
    dt_t = dt_p.T
    cum_q = cum[:L]
    last = cum[LS - 1:LS, :]

    erow = lax.broadcasted_iota(jnp.int32, (LANE, SSD_GW), 0)
    ecol = lax.broadcasted_iota(jnp.int32, (LANE, SSD_GW), 1) // SSD_HEADDIM
    expand = jnp.where(erow == ecol, 1.0, 0.0).astype(BF16)
    e_cum = _dot_exact_lhs(jnp.exp(cum_q), expand)
    w_dec = _dot_exact_lhs(jnp.exp(last - cum) * dt_p, expand)
    e_last = _dot_exact_lhs(jnp.exp(last), expand)

    cm_b = cm.astype(BF16)
    cb = _dot_nt(cm_b, bm_p)
    causal = lax.broadcasted_iota(jnp.int32, (L, LS), 0) >= lax.broadcasted_iota(jnp.int32, (L, LS), 1)
    lane_half = lax.broadcasted_iota(jnp.int32, (LS, LANE), 1) < SSD_HEADDIM

    def head_weights(r):
        diff = cum_q[:, r:r + 1] - cum_t[r:r + 1, :]
        return (cb * jnp.exp(jnp.where(causal, diff, -jnp.inf)) * dt_t[r:r + 1, :]).astype(BF16)

    y_parts = []
    for p in range(SSD_REP // 2):
        w_pair = jnp.concatenate([head_weights(2 * p), head_weights(2 * p + 1)], axis=1)
        x_pair = xs_p[:, p * LANE:(p + 1) * LANE]
        x_bd = jnp.concatenate([jnp.where(lane_half, x_pair, 0.0), jnp.where(lane_half, 0.0, x_pair)], axis=0)
        y_parts.append(_dot(w_pair, x_bd.astype(BF16)))
    y = jnp.concatenate(y_parts, axis=1)

    s_old = s_ref[...]
    y = y + _dot(cm_b, s_old.astype(BF16)) * e_cum
    s_new = s_old * e_last + _dot_tn(bm_p, (xs_p * w_dec).astype(BF16))
    s_ref[...] = s_new

    y = y + xs * dsk_ref[...]
    y = y * _silu(z_ref[...])
    y_ref[...] = _rms(y, gn_ref[...]).astype(y_ref.dtype)

    @pl.when(c == pl.num_programs(2) - 1)
    def _():
        s_out_ref[0] = s_new.T.reshape(SSD_REP, SSD_HEADDIM, SSD_N)


def _ssd(proj, row0, n_seq, seq_len, chunk, conv_buf, s0, w_conv, b_conv, dt_bias, a_log, d_skip, g_norm, dst=None):
    rows = proj.shape[0]
    d_inner = SSD_GROUPS * SSD_GW
    L = chunk
    n_chunks = seq_len // L
    base = row0 // L
    bn = SSD_GROUPS * SSD_N
    off_x, off_b, off_c, off_dt = d_inner, 2 * d_inner, 2 * d_inner + bn, 2 * d_inner + 2 * bn

    def rmap(col_blocks_off):
        return lambda b, g, c: (base + b * n_chunks + c, col_blocks_off + g)

    nb = SSD_CONV - 1
    in_specs = [
        pl.BlockSpec((L, SSD_GW), rmap(0)),
        pl.BlockSpec((L, SSD_GW), rmap(off_x // SSD_GW)),
        pl.BlockSpec((L, SSD_N), rmap(off_b // SSD_N)),
        pl.BlockSpec((L, SSD_N), rmap(off_c // SSD_N)),
        pl.BlockSpec((L, LANE), rmap(off_dt // LANE)),
        pl.BlockSpec((1, nb, SSD_GW), lambda b, g, c: (b, 0, g)),
        pl.BlockSpec((1, nb, SSD_N), lambda b, g, c: (b, 0, d_inner // SSD_N + g)),
        pl.BlockSpec((1, nb, SSD_N), lambda b, g, c: (b, 0, (d_inner + bn) // SSD_N + g)),
        pl.BlockSpec((SSD_CONV, SSD_GW), lambda b, g, c: (0, g)),
        pl.BlockSpec((SSD_CONV, SSD_N), lambda b, g, c: (0, d_inner // SSD_N + g)),
        pl.BlockSpec((SSD_CONV, SSD_N), lambda b, g, c: (0, (d_inner + bn) // SSD_N + g)),
        pl.BlockSpec((1, SSD_GW), lambda b, g, c: (0, g)),
        pl.BlockSpec((1, SSD_N), lambda b, g, c: (0, d_inner // SSD_N + g)),
        pl.BlockSpec((1, SSD_N), lambda b, g, c: (0, (d_inner + bn) // SSD_N + g)),
        pl.BlockSpec((1, 1, LANE), lambda b, g, c: (g, 0, 0)),
        pl.BlockSpec((1, 1, LANE), lambda b, g, c: (g, 0, 0)),
        pl.BlockSpec((1, SSD_GW), lambda b, g, c: (0, g)),
        pl.BlockSpec((1, SSD_GW), lambda b, g, c: (0, g)),
        pl.BlockSpec((1, SSD_REP, SSD_HEADDIM, SSD_N), lambda b, g, c: (b, g, 0, 0)),
    ]
    args = [proj] * 5 + [conv_buf] * 3 + [w_conv] * 3 + [b_conv] * 3 + [dt_bias, a_log, d_skip, g_norm, s0]
    aliases = {}
    if dst is not None:
        in_specs.append(pl.BlockSpec(memory_space=pl.ANY))
        args.append(dst)
        aliases = {len(args) - 1: 0}
    return pl.pallas_call(
        functools.partial(_ssd_body, aliased=dst is not None),
        grid=(n_seq, SSD_GROUPS, n_chunks),
        in_specs=in_specs,
        out_specs=[
            pl.BlockSpec((L, SSD_GW), lambda b, g, c: (base + b * n_chunks + c, g)),
            pl.BlockSpec((1, SSD_REP, SSD_HEADDIM, SSD_N), lambda b, g, c: (b, g, 0, 0)),
        ],
        out_shape=[
            jax.ShapeDtypeStruct((rows, d_inner), BF16),
            jax.ShapeDtypeStruct((n_seq, SSD_GROUPS * SSD_REP, SSD_HEADDIM, SSD_N), F32),
        ],
        scratch_shapes=[
            pltpu.VMEM((SSD_N, SSD_GW), F32),
            pltpu.VMEM((L + 8, SSD_GW), F32),
            pltpu.VMEM((L + 8, SSD_N), F32),
            pltpu.VMEM((L + 8, SSD_N), F32),
        ],
        input_output_aliases=aliases,
        compiler_params=_params("parallel", "parallel", "arbitrary"),
        name="ssd",
    )(*args)


def _rot_cols(w):
    half = w.shape[-1] // 2
    return jnp.concatenate([-w[..., half:], w[..., :half]], axis=-1)


def _prep_ab_in(w):
    d = w.shape[0]
    o = 0
    parts = {}
    for name, size in (("q", GLA_QK), ("k", GLA_QK), ("v", GLA_VW), ("gr", GLA_GATE_RANK), ("og", GLA_VW),
                       ("cq", MLA_Q_LORA), ("ckv", MLA_KV_LORA), ("kpe", MLA_ROPE)):
        parts[name] = w[:, o:o + size]
        o += size
    zeros = jnp.zeros((d, LANE - GLA_GATE_RANK), w.dtype)
    out = jnp.concatenate([parts["q"], parts["k"], parts["v"], parts["og"], parts["cq"], parts["ckv"],
                           parts["kpe"], _rot_cols(parts["kpe"]), parts["gr"], zeros], axis=1)
    assert out.shape[1] == AB_W
    return out.astype(BF16)


def _prep_uq(w):
    w = w.reshape(w.shape[0], MLA_HEADS, MLA_NOPE + MLA_ROPE)
    pe = w[..., MLA_NOPE:]
    out = jnp.concatenate([w[..., :MLA_NOPE], pe, _rot_cols(pe)], axis=-1)
    return out.reshape(w.shape[0], MLA_HEADS * MLA_QKW).astype(BF16)


def _prep_ssd_in(w, d_inner):
    d = w.shape[0]
    n_heads = SSD_GROUPS * SSD_REP
    main = w[:, :2 * d_inner + 2 * SSD_GROUPS * SSD_N]
    dt = w[:, 2 * d_inner + 2 * SSD_GROUPS * SSD_N:].reshape(d, SSD_GROUPS, SSD_REP)
    assert dt.shape[1] * dt.shape[2] == n_heads
    dt = jnp.concatenate([dt, jnp.zeros((d, SSD_GROUPS, LANE - SSD_REP), w.dtype)], axis=-1)
    return jnp.concatenate([main, dt.reshape(d, SSD_GROUPS * LANE)], axis=1).astype(BF16)


def _head_table(v):
    v = v.reshape(SSD_GROUPS, 1, SSD_REP)
    return jnp.concatenate([v, jnp.zeros((SSD_GROUPS, 1, LANE - SSD_REP), v.dtype)], axis=-1)


def _rope_tables(pos):
    half = MLA_ROPE // 2
    inv = ROPE_THETA ** (-jnp.arange(half, dtype=F32) / half)
    ang = pos.astype(F32)[:, None] * inv[None, :]
    zeros = jnp.zeros((pos.shape[0], LANE - MLA_ROPE), F32)
    cos = jnp.cos(ang)
    sin = jnp.sin(ang)
    return jnp.concatenate([cos, cos, zeros], axis=1), jnp.concatenate([sin, sin, zeros], axis=1)


def kernel(x_prompt, x_sample, cache_mla_ckv, cache_mla_kpe, state_gla, state_ssd, state_ssd_conv, g_ffn1, w_ffn1_gate, w_ffn1_up, w_ffn1_down, g_mix, g_ffn2, w_ffn2_gate, w_ffn2_up, w_ffn2_down, w_ab_in, w_gla_gate_up, b_gla_gate, g_gla_norm, g_mla_q_norm, w_mla_uq, g_mla_kv_norm, w_mla_uk, w_mla_uv, w_ab_out, w_ssd_in, w_ssd_conv, b_ssd_conv, ssd_dt_bias, ssd_a_log, ssd_d, g_ssd_norm, w_ssd_out, g_final):
    n_p, t_p, d = x_prompt.shape
    n_s, t_s, _ = x_sample.shape
    assert n_p == 1
    depth = g_ffn1.shape[0]
    past_len = cache_mla_ckv.shape[2]
    rows_p = n_p * t_p
    d_inner = w_ssd_out.shape[1]
    assert d_inner == SSD_GROUPS * SSD_GW

    x = jnp.concatenate([x_prompt.reshape(rows_p, d), x_sample.reshape(n_s * t_s, d)], axis=0)
    rows = rows_p + n_s * t_s
    pos = jnp.concatenate([jnp.arange(t_p, dtype=jnp.int32),
                           jnp.tile(past_len + jnp.arange(t_s, dtype=jnp.int32), n_s)])
    cos_t, sin_t = _rope_tables(pos)
    row2 = lambda v: v.reshape(1, -1)

    outs = {k: [] for k in ("p_ckv", "p_kpe", "p_gla", "p_ssd", "p_conv", "s_ckv", "s_kpe", "s_gla", "s_ssd", "s_conv")}
    for layer in range(depth):
        i = layer // 2
        x = _ffn(x, row2(g_ffn1[layer]), w_ffn1_gate, w_ffn1_up, w_ffn1_down, layer)
        if layer % 2 == 0:
            proj = _norm_proj(x, row2(g_mix[layer]), _prep_ab_in(w_ab_in[i]), AB_W // 2)
            w_gate = jnp.concatenate([w_gla_gate_up[i], jnp.zeros((LANE - GLA_GATE_RANK, GLA_QK), F32)], axis=0).astype(BF16)
            gla_args = (w_gate, row2(b_gla_gate[i]), row2(g_gla_norm[i]))
            gla_chunk = min(CHUNK, t_p)
            o_gla, p_gla = _gla(proj, 0, n_p, t_p, gla_chunk, min(4, t_p // gla_chunk),
                                jnp.zeros((n_p, GLA_HEADS, GLA_DK, GLA_DV), F32), *gla_args,
                                dst=jnp.zeros((rows, GLA_VW), BF16))
            o_gla, s_gla = _gla(proj, rows_p, n_s, t_s, t_s, 1, state_gla[i], *gla_args, dst=o_gla)

            w_uk = w_mla_uk[i].astype(BF16)
            w_uv = w_mla_uv[i].astype(BF16)
            q, k, vt, ckv_n, kpe_r = _mla_pre(proj, cos_t, sin_t, row2(g_mla_q_norm[i]), row2(g_mla_kv_norm[i]),
                                              _prep_uq(w_mla_uq[i]), w_uk, w_uv.T)
            o_mla = _flash(q, k, vt, t_p, 512, jnp.zeros((rows, MLA_HEADS * MLA_V), BF16))
            o_mla = _mla_hist(q, k, ckv_n, rows_p, n_s, t_s, cache_mla_ckv[i], cache_mla_kpe[i], w_uk, w_uv, o_mla)

            w_out = w_ab_out[i].astype(BF16)
            x = _out_proj(x, [o_gla, o_mla], [w_out[:GLA_VW], w_out[GLA_VW:]])
            outs["p_ckv"].append(ckv_n[:rows_p].reshape(n_p, t_p, MLA_KV_LORA))
            outs["s_ckv"].append(ckv_n[rows_p:].reshape(n_s, t_s, MLA_KV_LORA))
            outs["p_kpe"].append(kpe_r[:rows_p, :MLA_ROPE].reshape(n_p, t_p, MLA_ROPE))
            outs["s_kpe"].append(kpe_r[rows_p:, :MLA_ROPE].reshape(n_s, t_s, MLA_ROPE))
            outs["p_gla"].append(p_gla)
            outs["s_gla"].append(s_gla)
        else:
            n_heads = SSD_GROUPS * SSD_REP
            conv_ch = d_inner + 2 * SSD_GROUPS * SSD_N
            proj = _norm_proj(x, row2(g_mix[layer]), _prep_ssd_in(w_ssd_in[i], d_inner),
                              (2 * d_inner + 2 * SSD_GROUPS * SSD_N + SSD_GROUPS * LANE) // 8)
            ssd_args = (w_ssd_conv[i], row2(b_ssd_conv[i]), _head_table(ssd_dt_bias[i]), _head_table(ssd_a_log[i]),
                        row2(jnp.repeat(ssd_d[i], SSD_HEADDIM)), row2(g_ssd_norm[i]))
            y, p_ssd = _ssd(proj, 0, n_p, t_p, min(SSD_L, t_p), jnp.zeros((n_p, SSD_CONV - 1, conv_ch), F32),
                            jnp.zeros((n_p, n_heads, SSD_HEADDIM, SSD_N), F32), *ssd_args,
                            dst=jnp.zeros((rows, d_inner), BF16))
            y, s_ssd = _ssd(proj, rows_p, n_s, t_s, t_s, state_ssd_conv[i], state_ssd[i], *ssd_args, dst=y)
            x = _out_proj(x, [y], [w_ssd_out[i].astype(BF16)])
            nb = SSD_CONV - 1
            outs["p_conv"].append(jnp.stack([
                lax.slice(proj, (b * t_p + t_p - nb, d_inner), (b * t_p + t_p, d_inner + conv_ch)) for b in range(n_p)]))
            outs["s_conv"].append(jnp.stack([
                lax.slice(proj, (rows_p + b * t_s + t_s - nb, d_inner), (rows_p + b * t_s + t_s, d_inner + conv_ch))
                for b in range(n_s)]))
            outs["p_ssd"].append(p_ssd)
            outs["s_ssd"].append(s_ssd)
        x = _ffn(x, row2(g_ffn2[layer]), w_ffn2_gate, w_ffn2_up, w_ffn2_down, layer,
                 g_final=row2(g_final) if layer == depth - 1 else None)

    y_prompt = x[:rows_p].reshape(n_p, t_p, d)
    y_sample = x[rows_p:].reshape(n_s, t_s, d)
    st = {k: jnp.stack(v) for k, v in outs.items()}
    return (y_prompt, y_sample, st["p_ckv"], st["p_kpe"], st["p_gla"], st["p_ssd"], st["p_conv"],
            st["s_ckv"], st["s_kpe"], st["s_gla"], st["s_ssd"], st["s_conv"])
```

```python
import functools

import jax
import jax.numpy as jnp
from jax import lax
from jax.experimental import pallas as pl
from jax.experimental.pallas import tpu as pltpu

F32 = jnp.float32
BF16 = jnp.bfloat16

RMS_EPS = 1e-6
CHUNK = 64

GLA_HEADS = 4
GLA_DK = 128
GLA_DV = 256
GLA_GATE_RANK = 16
GLA_TAU = 16.0
GLA_QK = GLA_HEADS * GLA_DK
GLA_VW = GLA_HEADS * GLA_DV

MLA_HEADS = 8
MLA_Q_LORA = 512
MLA_KV_LORA = 512
MLA_NOPE = 128
MLA_ROPE = 64
MLA_V = 128
ROPE_THETA = 10000.0
MLA_QKW = 256
MLA_SCORE_SCALE = (MLA_NOPE + MLA_ROPE) ** -0.5 * 1.4426950408889634
MLA_HEADS_PER_STEP = 2

SSD_HEADDIM = 64
SSD_GROUPS = 8
SSD_N = 128
SSD_CONV = 4
SSD_GW = 512
SSD_REP = SSD_GW // SSD_HEADDIM

LANE = 128
AB_Q, AB_K, AB_V, AB_OG, AB_CQ, AB_CKV, AB_KPE, AB_GR, AB_W = 0, 512, 1024, 2048, 3072, 3584, 4096, 4224, 4352

VMEM_LIMIT = 56 * 1024 * 1024


def _tile(n, target, align):
    best = None
    for t in range(align, min(n, target) + 1, align):
        if n % t == 0:
            best = t
    return best if best is not None else n


def _rms(x, g):
    return x * lax.rsqrt(jnp.mean(x * x, axis=-1, keepdims=True) + RMS_EPS) * g


def _silu(x):
    return x * jax.nn.sigmoid(x)


def _softplus(x):
    return jnp.maximum(x, 0.0) + jnp.log1p(jnp.exp(-jnp.abs(x)))


def _dot(a, b):
    return jnp.dot(a, b, preferred_element_type=F32)


def _dot_nt(a, b):
    return lax.dot_general(a, b, (((1,), (1,)), ((), ())), preferred_element_type=F32)


def _dot_tn(a, b):
    return lax.dot_general(a, b, (((0,), (0,)), ((), ())), preferred_element_type=F32)


def _split3(x):
    h = x.astype(BF16)
    r = x - h.astype(F32)
    m = r.astype(BF16)
    l = (r - m.astype(F32)).astype(BF16)
    return h, m, l


def _dot_exact_rhs(a_bf16, x):
    h, m, l = _split3(x)
    return _dot(a_bf16, h) + _dot(a_bf16, m) + _dot(a_bf16, l)


def _dot_exact_lhs(x, b_bf16):
    h, m, l = _split3(x)
    return _dot(h, b_bf16) + _dot(m, b_bf16) + _dot(l, b_bf16)


def _params(*sem):
    return pltpu.CompilerParams(dimension_semantics=sem, vmem_limit_bytes=VMEM_LIMIT)


def _ffn_body(x_ref, g_ref, wg_ref, wu_ref, wd_ref, gf_ref, o_ref, h_ref, *, final_norm):
    j = pl.program_id(1)

    @pl.when(j == 0)
    def _():
        x = x_ref[...]
        h_ref[...] = _rms(x, g_ref[...]).astype(BF16)
        o_ref[...] = x

    h = h_ref[...]
    a = _dot(h, wg_ref[...].astype(BF16))
    b = _dot(h, wu_ref[...].astype(BF16))
    act = (_silu(a) * b) * 0.5
    o_ref[...] += _dot(act.astype(BF16), wd_ref[...].astype(BF16))

    if final_norm:
        @pl.when(j == pl.num_programs(1) - 1)
        def _():
            o_ref[...] = _rms(o_ref[...], gf_ref[...])


def _ffn(x, g, wg, wu, wd, layer, g_final=None):
    rows, d = x.shape
    f = wg.shape[2]
    tm = _tile(rows, 1056, 8)
    tf = _tile(f, 256, LANE)
    final = g_final is not None
    return pl.pallas_call(
        functools.partial(_ffn_body, final_norm=final),
        grid=(rows // tm, f // tf),
        in_specs=[
            pl.BlockSpec((tm, d), lambda i, j: (i, 0), pipeline_mode=pl.Buffered(1)),
            pl.BlockSpec((1, d), lambda i, j: (0, 0)),
            pl.BlockSpec((None, d, tf), lambda i, j: (layer, 0, j)),
            pl.BlockSpec((None, d, tf), lambda i, j: (layer, 0, j)),
            pl.BlockSpec((None, tf, d), lambda i, j: (layer, j, 0)),
            pl.BlockSpec((1, d), lambda i, j: (0, 0)),
        ],
        out_specs=pl.BlockSpec((tm, d), lambda i, j: (i, 0)),
        out_shape=jax.ShapeDtypeStruct((rows, d), F32),
        scratch_shapes=[pltpu.VMEM((tm, d), BF16)],
        compiler_params=_params("parallel", "arbitrary"),
        name="ffn",
    )(x, g, wg, wu, wd, g_final if final else g)


def _norm_proj_body(x_ref, g_ref, w_ref, o_ref, h_ref):
    @pl.when(pl.program_id(1) == 0)
    def _():
        h_ref[...] = _rms(x_ref[...], g_ref[...]).astype(BF16)

    o_ref[...] = _dot(h_ref[...], w_ref[...])


def _norm_proj(x, g, w, tn):
    rows, d = x.shape
    n = w.shape[1]
    tm = _tile(rows, 768, 8)
    return pl.pallas_call(
        _norm_proj_body,
        grid=(rows // tm, n // tn),
        in_specs=[
            pl.BlockSpec((tm, d), lambda i, j: (i, 0)),
            pl.BlockSpec((1, d), lambda i, j: (0, 0)),
            pl.BlockSpec((d, tn), lambda i, j: (0, j)),
        ],
        out_specs=pl.BlockSpec((tm, tn), lambda i, j: (i, j)),
        out_shape=jax.ShapeDtypeStruct((rows, n), F32),
        scratch_shapes=[pltpu.VMEM((tm, d), BF16)],
        compiler_params=_params("parallel", "arbitrary"),
        name="norm_proj",
    )(x, g, w)


def _out_proj_body(*refs, n_in):
    x_ref = refs[0]
    o_ref = refs[1 + 2 * n_in]
    acc = x_ref[...]
    for i in range(n_in):
        acc = acc + _dot(refs[1 + i][...], refs[1 + n_in + i][...])
    o_ref[...] = acc


def _out_proj(x, ys, ws):
    rows, d = x.shape
    n_in = len(ys)
    tm = _tile(rows, 768, 8)
    tn = _tile(d, 1024, LANE)
    in_specs = [pl.BlockSpec((tm, tn), lambda i, j: (i, j))]
    in_specs += [pl.BlockSpec((tm, y.shape[1]), lambda i, j: (i, 0)) for y in ys]
    in_specs += [pl.BlockSpec((w.shape[0], tn), lambda i, j: (0, j)) for w in ws]
    return pl.pallas_call(
        functools.partial(_out_proj_body, n_in=n_in),
        grid=(rows // tm, d // tn),
        in_specs=in_specs,
        out_specs=pl.BlockSpec((tm, tn), lambda i, j: (i, j)),
        out_shape=jax.ShapeDtypeStruct((rows, d), F32),
        compiler_params=_params("parallel", "arbitrary"),
        name="out_proj",
    )(x, *ys, *ws)


def _gla_body(q_ref, k_ref, v_ref, og_ref, gr_ref, s0_ref, wg_ref, bg_ref, gn_ref, *rest, aliased, L):
    o_ref, s_out_ref, s_ref = rest[1:] if aliased else rest
    step = pl.program_id(1)
    n_sub = q_ref.shape[0] // L

    @pl.when(step == 0)
    def _():
        for h in range(GLA_HEADS):
            s_ref[h] = s0_ref[0, h].T

    pre = _dot(gr_ref[...].astype(BF16), wg_ref[...]) + bg_ref[...]
    log_a = (jnp.minimum(pre, 0.0) - jnp.log1p(jnp.exp(-jnp.abs(pre)))) * (1.0 / GLA_TAU)
    causal = lax.broadcasted_iota(jnp.int32, (L, L), 0) >= lax.broadcasted_iota(jnp.int32, (L, L), 1)
    tri = jnp.where(causal, 1.0, 0.0).astype(BF16)
    gn = gn_ref[...]

    states = [s_ref[h] for h in range(GLA_HEADS)]
    for c in range(n_sub):
        r0 = c * L
        b_all = _dot_exact_rhs(tri, log_a[r0:r0 + L, :])
        for h in range(GLA_HEADS):
            kc = slice(h * GLA_DK, (h + 1) * GLA_DK)
            vc = slice(h * GLA_DV, (h + 1) * GLA_DV)
            b = b_all[:, kc]
            b_mid = b[L // 2 - 1:L // 2, :]
            b_last = b[L - 1:L, :]
            q = q_ref[r0:r0 + L, kc] * (GLA_DK ** -0.5)
            k = k_ref[r0:r0 + L, kc]
            v = v_ref[r0:r0 + L, vc].astype(BF16)
            q_in = (q * jnp.exp(b)).astype(BF16)
            q_mid = (q * jnp.exp(b - b_mid)).astype(BF16)
            k_mid = (k * jnp.exp(b_mid - b)).astype(BF16)
            k_end = (k * jnp.exp(b_last - b)).astype(BF16)
            s_old = states[h]
            att = jnp.where(causal, _dot_nt(q_mid, k_mid), 0.0)
            o = _dot(att.astype(BF16), v) + _dot_nt(q_in, s_old.astype(BF16))
            states[h] = s_old * jnp.exp(b_last) + _dot_tn(v, k_end)
            o_ref[r0:r0 + L, vc] = (_rms(o, gn) * _silu(og_ref[r0:r0 + L, vc])).astype(o_ref.dtype)

    for h in range(GLA_HEADS):
        s_ref[h] = states[h]

    @pl.when(step == pl.num_programs(1) - 1)
    def _():
        for h in range(GLA_HEADS):
            s_out_ref[0, h] = states[h].T


def _gla(proj, row0, n_seq, seq_len, chunk, n_sub, s0, w_gate, b_gate, g_norm, dst=None):
    rows = proj.shape[0]
    tr = chunk * n_sub
    n_steps = seq_len // tr
    base = row0 // tr

    def rmap(col_block):
        return lambda b, s: (base + b * n_steps + s, col_block)

    const = lambda b, s: (0, 0)
    in_specs = [
        pl.BlockSpec((tr, GLA_QK), rmap(AB_Q // GLA_QK)),
        pl.BlockSpec((tr, GLA_QK), rmap(AB_K // GLA_QK)),
        pl.BlockSpec((tr, GLA_VW), rmap(AB_V // GLA_VW)),
        pl.BlockSpec((tr, GLA_VW), rmap(AB_OG // GLA_VW)),
        pl.BlockSpec((tr, LANE), rmap(AB_GR // LANE)),
        pl.BlockSpec((1, GLA_HEADS, GLA_DK, GLA_DV), lambda b, s: (b, 0, 0, 0)),
        pl.BlockSpec((LANE, GLA_QK), const),
        pl.BlockSpec((1, GLA_QK), const),
        pl.BlockSpec((1, GLA_DV), const),
    ]
    args = [proj, proj, proj, proj, proj, s0, w_gate, b_gate, g_norm]
    aliases = {}
    if dst is not None:
        in_specs.append(pl.BlockSpec(memory_space=pl.ANY))
        args.append(dst)
        aliases = {len(args) - 1: 0}
    return pl.pallas_call(
        functools.partial(_gla_body, aliased=dst is not None, L=chunk),
        grid=(n_seq, n_steps),
        in_specs=in_specs,
        out_specs=[
            pl.BlockSpec((tr, GLA_VW), rmap(0)),
            pl.BlockSpec((1, GLA_HEADS, GLA_DK, GLA_DV), lambda b, s: (b, 0, 0, 0)),
        ],
        out_shape=[
            jax.ShapeDtypeStruct((rows, GLA_VW), BF16),
            jax.ShapeDtypeStruct((n_seq, GLA_HEADS, GLA_DK, GLA_DV), F32),
        ],
        scratch_shapes=[pltpu.VMEM((GLA_HEADS, GLA_DV, GLA_DK), F32)],
        input_output_aliases=aliases,
        compiler_params=_params("parallel", "arbitrary"),
        name="gla",
    )(*args)


def _rope128(x, cos_t, sin_t):
    return x * cos_t + pltpu.roll(x, MLA_ROPE, 1) * sin_t


def _mla_pre_body(cq_ref, ckv_ref, kpe_ref, cos_ref, sin_ref, gq_ref, gkv_ref, wuq_ref, wuk_ref, wuvt_ref,
                  q_ref, k_ref, vt_ref, ckvn_ref, kper_ref):
    scale = MLA_SCORE_SCALE
    cos_t = cos_ref[...]
    sin_t = sin_ref[...]
    qf = _dot(_rms(cq_ref[...], gq_ref[...]).astype(BF16), wuq_ref[...])
    ckv_n = _rms(ckv_ref[...], gkv_ref[...])
    ckvn_ref[...] = ckv_n
    kpe_r = _rope128(kpe_ref[...], cos_t, sin_t)
    kper_ref[...] = kpe_r
    kpe_b = kpe_r.astype(BF16)
    ckv_b = ckv_n.astype(BF16)
    k_nope = _dot(ckv_b, wuk_ref[...])
    vt_ref[...] = _dot_nt(wuvt_ref[...], ckv_b).astype(BF16)
    for h in range(MLA_HEADS):
        o = h * MLA_QKW
        q_ref[:, o:o + MLA_NOPE] = (qf[:, o:o + MLA_NOPE] * scale).astype(BF16)
        q_ref[:, o + MLA_NOPE:o + MLA_QKW] = (_rope128(qf[:, o + MLA_NOPE:o + MLA_QKW], cos_t, sin_t) * scale).astype(BF16)
        k_ref[:, o:o + MLA_NOPE] = k_nope[:, h * MLA_NOPE:(h + 1) * MLA_NOPE].astype(BF16)
        k_ref[:, o + MLA_NOPE:o + MLA_QKW] = kpe_b


def _mla_pre(proj, cos_t, sin_t, g_q, g_kv, w_uq, w_uk, w_uv_t):
    rows = proj.shape[0]
    tm = _tile(rows, 256, LANE)
    qkw = MLA_HEADS * MLA_QKW
    vw = MLA_HEADS * MLA_V
    row = lambda i: (i, 0)
    const = lambda i: (0, 0)
    return pl.pallas_call(
        _mla_pre_body,
        grid=(rows // tm,),
        in_specs=[
            pl.BlockSpec((tm, MLA_Q_LORA), lambda i: (i, AB_CQ // MLA_Q_LORA)),
            pl.BlockSpec((tm, MLA_KV_LORA), lambda i: (i, AB_CKV // MLA_KV_LORA)),
            pl.BlockSpec((tm, LANE), lambda i: (i, AB_KPE // LANE)),
            pl.BlockSpec((tm, LANE), row),
            pl.BlockSpec((tm, LANE), row),
            pl.BlockSpec((1, MLA_Q_LORA), const),
            pl.BlockSpec((1, MLA_KV_LORA), const),
            pl.BlockSpec((MLA_Q_LORA, qkw), const),
            pl.BlockSpec((MLA_KV_LORA, MLA_HEADS * MLA_NOPE), const),
            pl.BlockSpec((vw, MLA_KV_LORA), const),
        ],
        out_specs=[
            pl.BlockSpec((tm, qkw), row),
            pl.BlockSpec((tm, qkw), row),
            pl.BlockSpec((vw, tm), lambda i: (0, i)),
            pl.BlockSpec((tm, MLA_KV_LORA), row),
            pl.BlockSpec((tm, LANE), row),
        ],
        out_shape=[
            jax.ShapeDtypeStruct((rows, qkw), BF16),
            jax.ShapeDtypeStruct((rows, qkw), BF16),
            jax.ShapeDtypeStruct((vw, rows), BF16),
            jax.ShapeDtypeStruct((rows, MLA_KV_LORA), F32),
            jax.ShapeDtypeStruct((rows, LANE), F32),
        ],
        compiler_params=_params("parallel"),
        name="mla_pre",
    )(proj, proj, proj, cos_t, sin_t, g_q, g_kv, w_uq, w_uk, w_uv_t)


def _flash_body(q_ref, k_ref, vt_ref, dst_ref, o_ref, m_ref, l_ref, acc_ref, *, blk):
    del dst_ref
    qi = pl.program_id(1)
    n_heads = q_ref.shape[1] // MLA_QKW
    m_ref[...] = jnp.full(m_ref.shape, -jnp.inf, F32)
    l_ref[...] = jnp.zeros(l_ref.shape, F32)
    acc_ref[...] = jnp.zeros(acc_ref.shape, F32)

    def block(ki, masked):
        start = pl.multiple_of(ki * blk, blk)
        for h in range(n_heads):
            qk = slice(h * MLA_QKW, (h + 1) * MLA_QKW)
            st = _dot_nt(k_ref[pl.ds(start, blk), qk], q_ref[:, qk])
            if masked:
                k_chunk = lax.broadcasted_iota(jnp.int32, (blk, blk), 0) // CHUNK
                q_chunk = lax.broadcasted_iota(jnp.int32, (blk, blk), 1) // CHUNK
                st = jnp.where(k_chunk <= q_chunk, st, -jnp.inf)
            m_old = m_ref[h]
            m_new = jnp.maximum(m_old, jnp.max(st, axis=0, keepdims=True))
            alpha = jnp.exp2(m_old - m_new)
            p = jnp.exp2(st - m_new)
            l_ref[h] = alpha * l_ref[h] + jnp.sum(p, axis=0, keepdims=True)
            vt = vt_ref[h * MLA_V:(h + 1) * MLA_V, pl.ds(start, blk)]
            acc_ref[h] = alpha * acc_ref[h] + _dot(vt, p.astype(BF16))
            m_ref[h] = m_new

    def interior(ki, carry):
        block(ki, False)
        return carry

    lax.fori_loop(0, qi, interior, 0)
    block(qi, True)
    for h in range(n_heads):
        o_ref[:, h * MLA_V:(h + 1) * MLA_V] = (acc_ref[h] / l_ref[h]).T.astype(o_ref.dtype)


def _flash(q, k, vt, seq_len, tile, dst):
    blk = _tile(seq_len, tile, LANE)
    assert blk % CHUNK == 0
    hp = MLA_HEADS_PER_STEP
    return pl.pallas_call(
        functools.partial(_flash_body, blk=blk),
        grid=(MLA_HEADS // hp, seq_len // blk),
        in_specs=[
            pl.BlockSpec((blk, hp * MLA_QKW), lambda h, qi: (qi, h)),
            pl.BlockSpec((seq_len, hp * MLA_QKW), lambda h, qi: (0, h)),
            pl.BlockSpec((hp * MLA_V, seq_len), lambda h, qi: (h, 0)),
            pl.BlockSpec(memory_space=pl.ANY),
        ],
        out_specs=pl.BlockSpec((blk, hp * MLA_V), lambda h, qi: (qi, h)),
        out_shape=jax.ShapeDtypeStruct(dst.shape, dst.dtype),
        scratch_shapes=[pltpu.VMEM((hp, 1, blk), F32), pltpu.VMEM((hp, 1, blk), F32),
                        pltpu.VMEM((hp, MLA_V, blk), F32)],
        input_output_aliases={3: 0},
        compiler_params=_params("parallel", "arbitrary"),
        name="mla_flash",
    )(q, k, vt, dst)


def _mla_hist_body(q_ref, kn_ref, ckvn_ref, ckv_ref, kpe_ref, wuk_ref, wuv_ref, dst_ref, o_ref, kpe_pad_ref, *, past_len):
    del dst_ref
    t_new = q_ref.shape[0]
    v_new = _dot(ckvn_ref[...].astype(BF16), wuv_ref[...]).astype(BF16)
    ckv_b = ckv_ref[0].astype(BF16)
    k_past = _dot(ckv_b, wuk_ref[...]).astype(BF16)
    v_past = _dot(ckv_b, wuv_ref[...]).astype(BF16)
    kpe_pad_ref[:, :MLA_ROPE] = kpe_ref[0].astype(BF16)
    kpe_pad_ref[:, MLA_ROPE:] = jnp.zeros((past_len, LANE - MLA_ROPE), BF16)
    kpe_past = kpe_pad_ref[...]

    q_chunk = (past_len + lax.broadcasted_iota(jnp.int32, (t_new, past_len), 0)) // CHUNK
    mask_past = lax.broadcasted_iota(jnp.int32, (t_new, past_len), 1) // CHUNK <= q_chunk
    q_chunk_n = (past_len + lax.broadcasted_iota(jnp.int32, (t_new, t_new), 0)) // CHUNK
    mask_new = (past_len + lax.broadcasted_iota(jnp.int32, (t_new, t_new), 1)) // CHUNK <= q_chunk_n

    for h in range(MLA_HEADS):
        o = h * MLA_QKW
        q_nope = q_ref[:, o:o + MLA_NOPE]
        q_pe = q_ref[:, o + MLA_NOPE:o + MLA_QKW]
        s_past = _dot_nt(q_nope, k_past[:, h * MLA_NOPE:(h + 1) * MLA_NOPE]) + _dot_nt(q_pe, kpe_past)
        s_new = _dot_nt(q_ref[:, o:o + MLA_QKW], kn_ref[:, o:o + MLA_QKW])
        s_past = jnp.where(mask_past, s_past, -jnp.inf)
        s_new = jnp.where(mask_new, s_new, -jnp.inf)
        m = jnp.maximum(jnp.max(s_past, axis=-1, keepdims=True), jnp.max(s_new, axis=-1, keepdims=True))
        p_past = jnp.exp2(s_past - m)
        p_new = jnp.exp2(s_new - m)
        denom = jnp.sum(p_past, axis=-1, keepdims=True) + jnp.sum(p_new, axis=-1, keepdims=True)
        pv = _dot(p_past.astype(BF16), v_past[:, h * MLA_V:(h + 1) * MLA_V])
        pv = pv + _dot(p_new.astype(BF16), v_new[:, h * MLA_V:(h + 1) * MLA_V])
        o_ref[:, h * MLA_V:(h + 1) * MLA_V] = (pv / denom).astype(o_ref.dtype)


def _mla_hist(q, k, ckv_n, row0, n_seq, t_new, ckv_past, kpe_past, w_uk, w_uv, dst):
    past_len = ckv_past.shape[1]
    base = row0 // t_new
    qkw = MLA_HEADS * MLA_QKW
    vw = MLA_HEADS * MLA_V
    row = lambda b: (base + b, 0)
    const = lambda b: (0, 0)
    return pl.pallas_call(
        functools.partial(_mla_hist_body, past_len=past_len),
        grid=(n_seq,),
        in_specs=[
            pl.BlockSpec((t_new, qkw), row),
            pl.BlockSpec((t_new, qkw), row),
            pl.BlockSpec((t_new, MLA_KV_LORA), row),
            pl.BlockSpec((1, past_len, MLA_KV_LORA), lambda b: (b, 0, 0)),
            pl.BlockSpec((1, past_len, MLA_ROPE), lambda b: (b, 0, 0)),
            pl.BlockSpec((MLA_KV_LORA, MLA_HEADS * MLA_NOPE), const),
            pl.BlockSpec((MLA_KV_LORA, vw), const),
            pl.BlockSpec(memory_space=pl.ANY),
        ],
        out_specs=pl.BlockSpec((t_new, vw), row),
        out_shape=jax.ShapeDtypeStruct(dst.shape, dst.dtype),
        scratch_shapes=[pltpu.VMEM((past_len, LANE), BF16)],
        input_output_aliases={7: 0},
        compiler_params=_params("parallel"),
        name="mla_hist",
    )(q, k, ckv_n, ckv_past, kpe_past, w_uk, w_uv, dst)


SSD_L = 128
SSD_GROUPS_PER_STEP = 2


def _conv_silu(x_ref, buf_ref, w_ref, b_ref, ext_ref, first):
    L = x_ref.shape[0]

    @pl.when(first)
    def _():
        ext_ref[8 - (SSD_CONV - 1):8, :] = buf_ref[0]

    @pl.when(jnp.logical_not(first))
    def _():
        ext_ref[0:8, :] = ext_ref[L:L + 8, :]

    ext_ref[8:8 + L, :] = x_ref[...]
    y = b_ref[...]
    for j in range(SSD_CONV):
        y = y + ext_ref[8 - (SSD_CONV - 1) + j:8 - (SSD_CONV - 1) + j + L, :] * w_ref[j:j + 1, :]
    return _silu(y)


def _pad_rows(x, n):
    if x.shape[0] == n:
        return x
    return jnp.concatenate([x, jnp.zeros((n - x.shape[0], x.shape[1]), x.dtype)], axis=0)


def _ssd_body(z_ref, x_ref, bm_ref, cm_ref, dt_ref, bufx_ref, bufb_ref, bufc_ref, wx_ref, wb_ref, wc_ref,
              bx_ref, bb_ref, bc_ref, dtb_ref, alog_ref, dsk_ref, gn_ref, s0_ref, dst_ref,
              y_ref, s_out_ref, s_ref, extx_ref, extb_ref, extc_ref):
    del dst_ref
    c = pl.program_id(2)
    first = c == 0
    L = x_ref.shape[0]
    LS = SSD_L
    n_grp = x_ref.shape[1] // SSD_GW

    @pl.when(first)
    def _():
        for p in range(n_grp):
            s_ref[p] = s0_ref[0, p * SSD_REP:(p + 1) * SSD_REP].reshape(SSD_GW, SSD_N).T

    xs_all = _conv_silu(x_ref, bufx_ref, wx_ref, bx_ref, extx_ref, first)
    bm_all = _conv_silu(bm_ref, bufb_ref, wb_ref, bb_ref, extb_ref, first)
    cm_all = _conv_silu(cm_ref, bufc_ref, wc_ref, bc_ref, extc_ref, first)

    row = lax.broadcasted_iota(jnp.int32, (LS, LS), 0)
    col = lax.broadcasted_iota(jnp.int32, (LS, LS), 1)
    tri = jnp.where(row >= col, 1.0, 0.0).astype(BF16)
    erow = lax.broadcasted_iota(jnp.int32, (LANE, SSD_GW), 0)
    ecol = lax.broadcasted_iota(jnp.int32, (LANE, SSD_GW), 1) // SSD_HEADDIM
    expand = jnp.where(erow == ecol, 1.0, 0.0).astype(BF16)
    causal = lax.broadcasted_iota(jnp.int32, (L, LS), 0) >= lax.broadcasted_iota(jnp.int32, (L, LS), 1)
    lane_half = lax.broadcasted_iota(jnp.int32, (LS, LANE), 1) < SSD_HEADDIM

    new_states = []
    for p in range(n_grp):
        gw = slice(p * SSD_GW, (p + 1) * SSD_GW)
        gn = slice(p * SSD_N, (p + 1) * SSD_N)
        xs = xs_all[:, gw]
        cm_b = cm_all[:, gn].astype(BF16)
        dt = _softplus(dt_ref[:, p * LANE:(p + 1) * LANE] + dtb_ref[p])
        xs_p = _pad_rows(xs, LS)
        bm_p = _pad_rows(bm_all[:, gn], LS).astype(BF16)
        dt_p = _pad_rows(dt, LS)
        da = dt_p * (-jnp.exp(alog_ref[p]))

        cum = _dot_exact_rhs(tri, da)
        cum_t = cum.T
        dt_t = dt_p.T
        cum_q = cum[:L]
        last = cum[LS - 1:LS, :]
        e_cum = _dot_exact_lhs(jnp.exp(cum_q), expand)
        w_dec = _dot_exact_lhs(jnp.exp(last - cum) * dt_p, expand)
        e_last = _dot_exact_lhs(jnp.exp(last), expand)
        cb = _dot_nt(cm_b, bm_p)

        def head_weights(r):
            diff = cum_q[:, r:r + 1] - cum_t[r:r + 1, :]
            return (cb * jnp.exp(jnp.where(causal, diff, -jnp.inf)) * dt_t[r:r + 1, :]).astype(BF16)

        y_parts = []
        for q in range(SSD_REP // 2):
            w_pair = jnp.concatenate([head_weights(2 * q), head_weights(2 * q + 1)], axis=1)
            x_pair = xs_p[:, q * LANE:(q + 1) * LANE]
            x_bd = jnp.concatenate([jnp.where(lane_half, x_pair, 0.0), jnp.where(lane_half, 0.0, x_pair)], axis=0)
            y_parts.append(_dot(w_pair, x_bd.astype(BF16)))
        y = jnp.concatenate(y_parts, axis=1)

        s_old = s_ref[p]
        y = y + _dot(cm_b, s_old.astype(BF16)) * e_cum
        s_new = s_old * e_last + _dot_tn(bm_p, (xs_p * w_dec).astype(BF16))
        s_ref[p] = s_new
        new_states.append(s_new)

        y = y + xs * dsk_ref[:, gw]
        y = y * _silu(z_ref[:, gw])
        y_ref[:, gw] = _rms(y, gn_ref[:, gw]).astype(y_ref.dtype)

    @pl.when(c == pl.num_programs(2) - 1)
    def _():
        for p in range(n_grp):
            s_out_ref[0, p * SSD_REP:(p + 1) * SSD_REP] = new_states[p].T.reshape(SSD_REP, SSD_HEADDIM, SSD_N)


def _ssd(proj, row0, n_seq, seq_len, chunk, conv_buf, s0, w_conv, b_conv, dt_bias, a_log, d_skip, g_norm, dst):
    d_inner = SSD_GROUPS * SSD_GW
    L = chunk
    n_chunks = seq_len // L
    base = row0 // L
    bn = SSD_GROUPS * SSD_N
    gp = SSD_GROUPS_PER_STEP
    gw, gn, gl = gp * SSD_GW, gp * SSD_N, gp * LANE
    off_x, off_b, off_c, off_dt = d_inner, 2 * d_inner, 2 * d_inner + bn, 2 * d_inner + 2 * bn

    def rmap(col_blocks_off):
        return lambda b, g, c: (base + b * n_chunks + c, col_blocks_off + g)

    def cmap(col_blocks_off):
        return lambda b, g, c: (0, col_blocks_off + g)

    nb = SSD_CONV - 1
    in_specs = [
        pl.BlockSpec((L, gw), rmap(0)),
        pl.BlockSpec((L, gw), rmap(off_x // gw)),
        pl.BlockSpec((L, gn), rmap(off_b // gn)),
        pl.BlockSpec((L, gn), rmap(off_c // gn)),
        pl.BlockSpec((L, gl), rmap(off_dt // gl)),
        pl.BlockSpec((1, nb, gw), lambda b, g, c: (b, 0, g)),
        pl.BlockSpec((1, nb, gn), lambda b, g, c: (b, 0, d_inner // gn + g)),
        pl.BlockSpec((1, nb, gn), lambda b, g, c: (b, 0, (d_inner + bn) // gn + g)),
        pl.BlockSpec((SSD_CONV, gw), cmap(0)),
        pl.BlockSpec((SSD_CONV, gn), cmap(d_inner // gn)),
        pl.BlockSpec((SSD_CONV, gn), cmap((d_inner + bn) // gn)),
        pl.BlockSpec((1, gw), cmap(0)),
        pl.BlockSpec((1, gn), cmap(d_inner // gn)),
        pl.BlockSpec((1, gn), cmap((d_inner + bn) // gn)),
        pl.BlockSpec((gp, 1, LANE), lambda b, g, c: (g, 0, 0)),
        pl.BlockSpec((gp, 1, LANE), lambda b, g, c: (g, 0, 0)),
        pl.BlockSpec((1, gw), cmap(0)),
        pl.BlockSpec((1, gw), cmap(0)),
        pl.BlockSpec((1, gp * SSD_REP, SSD_HEADDIM, SSD_N), lambda b, g, c: (b, g, 0, 0)),
        pl.BlockSpec(memory_space=pl.ANY),
    ]
    args = [proj] * 5 + [conv_buf] * 3 + [w_conv] * 3 + [b_conv] * 3 + [dt_bias, a_log, d_skip, g_norm, s0, dst]
    return pl.pallas_call(
        _ssd_body,
        grid=(n_seq, SSD_GROUPS // gp, n_chunks),
        in_specs=in_specs,
        out_specs=[
            pl.BlockSpec((L, gw), rmap(0)),
            pl.BlockSpec((1, gp * SSD_REP, SSD_HEADDIM, SSD_N), lambda b, g, c: (b, g, 0, 0)),
        ],
        out_shape=[
            jax.ShapeDtypeStruct(dst.shape, dst.dtype),
            jax.ShapeDtypeStruct((n_seq, SSD_GROUPS * SSD_REP, SSD_HEADDIM, SSD_N), F32),
        ],
        scratch_shapes=[
            pltpu.VMEM((gp, SSD_N, SSD_GW), F32),
            pltpu.VMEM((L + 8, gw), F32),
            pltpu.VMEM((L + 8, gn), F32),
            pltpu.VMEM((L + 8, gn), F32),
        ],
        input_output_aliases={len(args) - 1: 0},
        compiler_params=_params("parallel", "parallel", "arbitrary"),
        name="ssd",
    )(*args)


def _rot_cols(w):
    half = w.shape[-1] // 2
    return jnp.concatenate([-w[..., half:], w[..., :half]], axis=-1)


def _prep_ab_in(w):
    d = w.shape[0]
    o = 0
    parts = {}
    for name, size in (("q", GLA_QK), ("k", GLA_QK), ("v", GLA_VW), ("gr", GLA_GATE_RANK), ("og", GLA_VW),
                       ("cq", MLA_Q_LORA), ("ckv", MLA_KV_LORA), ("kpe", MLA_ROPE)):
        parts[name] = w[:, o:o + size]
        o += size
    zeros = jnp.zeros((d, LANE - GLA_GATE_RANK), w.dtype)
    out = jnp.concatenate([parts["q"], parts["k"], parts["v"], parts["og"], parts["cq"], parts["ckv"],
                           parts["kpe"], _rot_cols(parts["kpe"]), parts["gr"], zeros], axis=1)
    assert out.shape[1] == AB_W
    return out.astype(BF16)


def _prep_uq(w):
    w = w.reshape(w.shape[0], MLA_HEADS, MLA_NOPE + MLA_ROPE)
    pe = w[..., MLA_NOPE:]
    out = jnp.concatenate([w[..., :MLA_NOPE], pe, _rot_cols(pe)], axis=-1)
    return out.reshape(w.shape[0], MLA_HEADS * MLA_QKW).astype(BF16)


def _prep_ssd_in(w, d_inner):
    d = w.shape[0]
    n_heads = SSD_GROUPS * SSD_REP
    main = w[:, :2 * d_inner + 2 * SSD_GROUPS * SSD_N]
    dt = w[:, 2 * d_inner + 2 * SSD_GROUPS * SSD_N:].reshape(d, SSD_GROUPS, SSD_REP)
    assert dt.shape[1] * dt.shape[2] == n_heads
    dt = jnp.concatenate([dt, jnp.zeros((d, SSD_GROUPS, LANE - SSD_REP), w.dtype)], axis=-1)
    return jnp.concatenate([main, dt.reshape(d, SSD_GROUPS * LANE)], axis=1).astype(BF16)


def _head_table(v):
    v = v.reshape(SSD_GROUPS, 1, SSD_REP)
    return jnp.concatenate([v, jnp.zeros((SSD_GROUPS, 1, LANE - SSD_REP), v.dtype)], axis=-1)


def _rope_tables(pos):
    half = MLA_ROPE // 2
    inv = ROPE_THETA ** (-jnp.arange(half, dtype=F32) / half)
    ang = pos.astype(F32)[:, None] * inv[None, :]
    zeros = jnp.zeros((pos.shape[0], LANE - MLA_ROPE), F32)
    cos = jnp.cos(ang)
    sin = jnp.sin(ang)
    return jnp.concatenate([cos, cos, zeros], axis=1), jnp.concatenate([sin, sin, zeros], axis=1)


def kernel(x_prompt, x_sample, cache_mla_ckv, cache_mla_kpe, state_gla, state_ssd, state_ssd_conv, g_ffn1, w_ffn1_gate, w_ffn1_up, w_ffn1_down, g_mix, g_ffn2, w_ffn2_gate, w_ffn2_up, w_ffn2_down, w_ab_in, w_gla_gate_up, b_gla_gate, g_gla_norm, g_mla_q_norm, w_mla_uq, g_mla_kv_norm, w_mla_uk, w_mla_uv, w_ab_out, w_ssd_in, w_ssd_conv, b_ssd_conv, ssd_dt_bias, ssd_a_log, ssd_d, g_ssd_norm, w_ssd_out, g_final):
    n_p, t_p, d = x_prompt.shape
    n_s, t_s, _ = x_sample.shape
    assert n_p == 1
    depth = g_ffn1.shape[0]
    past_len = cache_mla_ckv.shape[2]
    rows_p = n_p * t_p
    d_inner = w_ssd_out.shape[1]
    assert d_inner == SSD_GROUPS * SSD_GW

    x = jnp.concatenate([x_prompt.reshape(rows_p, d), x_sample.reshape(n_s * t_s, d)], axis=0)
    rows = rows_p + n_s * t_s
    pos = jnp.concatenate([jnp.arange(t_p, dtype=jnp.int32),
                           jnp.tile(past_len + jnp.arange(t_s, dtype=jnp.int32), n_s)])
    cos_t, sin_t = _rope_tables(pos)
    row2 = lambda v: v.reshape(1, -1)

    outs = {k: [] for k in ("p_ckv", "p_kpe", "p_gla", "p_ssd", "p_conv", "s_ckv", "s_kpe", "s_gla", "s_ssd", "s_conv")}
    for layer in range(depth):
        i = layer // 2
        x = _ffn(x, row2(g_ffn1[layer]), w_ffn1_gate, w_ffn1_up, w_ffn1_down, layer)
        if layer % 2 == 0:
            proj = _norm_proj(x, row2(g_mix[layer]), _prep_ab_in(w_ab_in[i]), AB_W // 2)
            w_gate = jnp.concatenate([w_gla_gate_up[i], jnp.zeros((LANE - GLA_GATE_RANK, GLA_QK), F32)], axis=0).astype(BF16)
            gla_args = (w_gate, row2(b_gla_gate[i]), row2(g_gla_norm[i]))
            gla_chunk = min(CHUNK, t_p)
            o_gla, p_gla = _gla(proj, 0, n_p, t_p, gla_chunk, min(4, t_p // gla_chunk),
                                jnp.zeros((n_p, GLA_HEADS, GLA_DK, GLA_DV), F32), *gla_args,
                                dst=jnp.zeros((rows, GLA_VW), BF16))
            o_gla, s_gla = _gla(proj, rows_p, n_s, t_s, t_s, 1, state_gla[i], *gla_args, dst=o_gla)

            w_uk = w_mla_uk[i].astype(BF16)
            w_uv = w_mla_uv[i].astype(BF16)
            q, k, vt, ckv_n, kpe_r = _mla_pre(proj, cos_t, sin_t, row2(g_mla_q_norm[i]), row2(g_mla_kv_norm[i]),
                                              _prep_uq(w_mla_uq[i]), w_uk, w_uv.T)
            o_mla = _flash(q, k, vt, t_p, 512, jnp.zeros((rows, MLA_HEADS * MLA_V), BF16))
            o_mla = _mla_hist(q, k, ckv_n, rows_p, n_s, t_s, cache_mla_ckv[i], cache_mla_kpe[i], w_uk, w_uv, o_mla)

            w_out = w_ab_out[i].astype(BF16)
            x = _out_proj(x, [o_gla, o_mla], [w_out[:GLA_VW], w_out[GLA_VW:]])
            outs["p_ckv"].append(ckv_n[:rows_p].reshape(n_p, t_p, MLA_KV_LORA))
            outs["s_ckv"].append(ckv_n[rows_p:].reshape(n_s, t_s, MLA_KV_LORA))
            outs["p_kpe"].append(kpe_r[:rows_p, :MLA_ROPE].reshape(n_p, t_p, MLA_ROPE))
            outs["s_kpe"].append(kpe_r[rows_p:, :MLA_ROPE].reshape(n_s, t_s, MLA_ROPE))
            outs["p_gla"].append(p_gla)
            outs["s_gla"].append(s_gla)
        else:
            n_heads = SSD_GROUPS * SSD_REP
            conv_ch = d_inner + 2 * SSD_GROUPS * SSD_N
            proj = _norm_proj(x, row2(g_mix[layer]), _prep_ssd_in(w_ssd_in[i], d_inner),
                              (2 * d_inner + 2 * SSD_GROUPS * SSD_N + SSD_GROUPS * LANE) // 8)
            ssd_args = (w_ssd_conv[i], row2(b_ssd_conv[i]), _head_table(ssd_dt_bias[i]), _head_table(ssd_a_log[i]),
                        row2(jnp.repeat(ssd_d[i], SSD_HEADDIM)), row2(g_ssd_norm[i]))
            y, p_ssd = _ssd(proj, 0, n_p, t_p, min(SSD_L, t_p), jnp.zeros((n_p, SSD_CONV - 1, conv_ch), F32),
                            jnp.zeros((n_p, n_heads, SSD_HEADDIM, SSD_N), F32), *ssd_args,
                            dst=jnp.zeros((rows, d_inner), BF16))
            y, s_ssd = _ssd(proj, rows_p, n_s, t_s, t_s, state_ssd_conv[i], state_ssd[i], *ssd_args, dst=y)
            x = _out_proj(x, [y], [w_ssd_out[i].astype(BF16)])
            nb = SSD_CONV - 1
            outs["p_conv"].append(jnp.stack([
                lax.slice(proj, (b * t_p + t_p - nb, d_inner), (b * t_p + t_p, d_inner + conv_ch)) for b in range(n_p)]))
            outs["s_conv"].append(jnp.stack([
                lax.slice(proj, (rows_p + b * t_s + t_s - nb, d_inner), (rows_p + b * t_s + t_s, d_inner + conv_ch))
                for b in range(n_s)]))
            outs["p_ssd"].append(p_ssd)
            outs["s_ssd"].append(s_ssd)
        x = _ffn(x, row2(g_ffn2[layer]), w_ffn2_gate, w_ffn2_up, w_ffn2_down, layer,
                 g_final=row2(g_final) if layer == depth - 1 else None)

    y_prompt = x[:rows_p].reshape(n_p, t_p, d)
    y_sample = x[rows_p:].reshape(n_s, t_s, d)
    st = {k: jnp.stack(v) for k, v in outs.items()}
    return (y_prompt, y_sample, st["p_ckv"], st["p_kpe"], st["p_gla"], st["p_ssd"], st["p_conv"],
            st["s_ckv"], st["s_kpe"], st["s_gla"], st["s_ssd"], st["s_conv"])
```

```python
import functools

import jax
import jax.numpy as jnp
from jax import lax
from jax.experimental import pallas as pl
from jax.experimental.pallas import tpu as pltpu

F32 = jnp.float32
BF16 = jnp.bfloat16

RMS_EPS = 1e-6
CHUNK = 64

GLA_HEADS = 4
GLA_DK = 128
GLA_DV = 256
GLA_GATE_RANK = 16
GLA_TAU = 16.0
GLA_SAFE_RANGE = 40.0
GLA_QK = GLA_HEADS * GLA_DK
GLA_VW = GLA_HEADS * GLA_DV

MLA_HEADS = 8
MLA_Q_LORA = 512
MLA_KV_LORA = 512
MLA_NOPE = 128
MLA_ROPE = 64
MLA_V = 128
ROPE_THETA = 10000.0
MLA_QKW = 256
MLA_SCORE_SCALE = (MLA_NOPE + MLA_ROPE) ** -0.5 * 1.4426950408889634
MLA_HEADS_PER_STEP = 2

SSD_HEADDIM = 64
SSD_GROUPS = 8
SSD_N = 128
SSD_CONV = 4
SSD_GW = 512
SSD_REP = SSD_GW // SSD_HEADDIM

LANE = 128
SUBLANE = 8
AB_Q, AB_K, AB_V, AB_OG, AB_CQ, AB_CKV, AB_KPE, AB_GR, AB_W = 0, 512, 1024, 2048, 3072, 3584, 4096, 4224, 4352

VMEM_LIMIT = 56 * 1024 * 1024


def _tile(n, target, align):
    best = None
    for t in range(align, min(n, target) + 1, align):
        if n % t == 0:
            best = t
    return best if best is not None else n


def _rms(x, g):
    return x * lax.rsqrt(jnp.mean(x * x, axis=-1, keepdims=True) + RMS_EPS) * g


def _silu(x):
    return x * jax.nn.sigmoid(x)


def _softplus(x):
    return jnp.maximum(x, 0.0) + jnp.log1p(jnp.exp(-jnp.abs(x)))


def _dot(a, b):
    return jnp.dot(a, b, preferred_element_type=F32)


def _dot_nt(a, b):
    return lax.dot_general(a, b, (((1,), (1,)), ((), ())), preferred_element_type=F32)


def _dot_tn(a, b):
    return lax.dot_general(a, b, (((0,), (0,)), ((), ())), preferred_element_type=F32)


def _split3(x):
    h = x.astype(BF16)
    r = x - h.astype(F32)
    m = r.astype(BF16)
    l = (r - m.astype(F32)).astype(BF16)
    return h, m, l


def _dot_exact_rhs(a_bf16, x):
    h, m, l = _split3(x)
    return _dot(a_bf16, h) + _dot(a_bf16, m) + _dot(a_bf16, l)


def _dot_exact_lhs(x, b_bf16):
    h, m, l = _split3(x)
    return _dot(h, b_bf16) + _dot(m, b_bf16) + _dot(l, b_bf16)


def _params(*sem):
    return pltpu.CompilerParams(dimension_semantics=sem, vmem_limit_bytes=VMEM_LIMIT)


def _ffn_body(x_ref, g_ref, wg_ref, wu_ref, wd_ref, gf_ref, o_ref, h_ref, *, final_norm):
    j = pl.program_id(1)

    @pl.when(j == 0)
    def _():
        x = x_ref[...]
        h_ref[...] = _rms(x, g_ref[...]).astype(BF16)
        o_ref[...] = x

    h = h_ref[...]
    a = _dot(h, wg_ref[...].astype(BF16))
    b = _dot(h, wu_ref[...].astype(BF16))
    act = (_silu(a) * b) * 0.5
    o_ref[...] += _dot(act.astype(BF16), wd_ref[...].astype(BF16))

    if final_norm:
        @pl.when(j == pl.num_programs(1) - 1)
        def _():
            o_ref[...] = _rms(o_ref[...], gf_ref[...])


def _ffn(x, g, wg, wu, wd, layer, g_final=None):
    rows, d = x.shape
    f = wg.shape[2]
    tm = _tile(rows, 1056, 8)
    tf = _tile(f, 256, LANE)
    final = g_final is not None
    return pl.pallas_call(
        functools.partial(_ffn_body, final_norm=final),
        grid=(rows // tm, f // tf),
        in_specs=[
            pl.BlockSpec((tm, d), lambda i, j: (i, 0), pipeline_mode=pl.Buffered(1)),
            pl.BlockSpec((1, d), lambda i, j: (0, 0)),
            pl.BlockSpec((None, d, tf), lambda i, j: (layer, 0, j)),
            pl.BlockSpec((None, d, tf), lambda i, j: (layer, 0, j)),
            pl.BlockSpec((None, tf, d), lambda i, j: (layer, j, 0)),
            pl.BlockSpec((1, d), lambda i, j: (0, 0)),
        ],
        out_specs=pl.BlockSpec((tm, d), lambda i, j: (i, 0)),
        out_shape=jax.ShapeDtypeStruct((rows, d), F32),
        scratch_shapes=[pltpu.VMEM((tm, d), BF16)],
        compiler_params=_params("parallel", "arbitrary"),
        name="ffn",
    )(x, g, wg, wu, wd, g_final if final else g)


def _norm_proj_body(x_ref, g_ref, w_ref, o_ref, h_ref):
    @pl.when(pl.program_id(1) == 0)
    def _():
        h_ref[...] = _rms(x_ref[...], g_ref[...]).astype(BF16)

    o_ref[...] = _dot(h_ref[...], w_ref[...])


def _norm_proj(x, g, w, tn):
    rows, d = x.shape
    n = w.shape[1]
    tm = _tile(rows, 768, 8)
    return pl.pallas_call(
        _norm_proj_body,
        grid=(rows // tm, n // tn),
        in_specs=[
            pl.BlockSpec((tm, d), lambda i, j: (i, 0)),
            pl.BlockSpec((1, d), lambda i, j: (0, 0)),
            pl.BlockSpec((d, tn), lambda i, j: (0, j)),
        ],
        out_specs=pl.BlockSpec((tm, tn), lambda i, j: (i, j)),
        out_shape=jax.ShapeDtypeStruct((rows, n), F32),
        scratch_shapes=[pltpu.VMEM((tm, d), BF16)],
        compiler_params=_params("parallel", "arbitrary"),
        name="norm_proj",
    )(x, g, w)


def _out_proj_body(*refs, n_in):
    x_ref = refs[0]
    o_ref = refs[1 + 2 * n_in]
    acc = x_ref[...]
    for i in range(n_in):
        acc = acc + _dot(refs[1 + i][...], refs[1 + n_in + i][...])
    o_ref[...] = acc


def _out_proj(x, ys, ws):
    rows, d = x.shape
    n_in = len(ys)
    tm = _tile(rows, 768, 8)
    tn = _tile(d, 1024, LANE)
    in_specs = [pl.BlockSpec((tm, tn), lambda i, j: (i, j))]
    in_specs += [pl.BlockSpec((tm, y.shape[1]), lambda i, j: (i, 0)) for y in ys]
    in_specs += [pl.BlockSpec((w.shape[0], tn), lambda i, j: (0, j)) for w in ws]
    return pl.pallas_call(
        functools.partial(_out_proj_body, n_in=n_in),
        grid=(rows // tm, d // tn),
        in_specs=in_specs,
        out_specs=pl.BlockSpec((tm, tn), lambda i, j: (i, j)),
        out_shape=jax.ShapeDtypeStruct((rows, d), F32),
        compiler_params=_params("parallel", "arbitrary"),
        name="out_proj",
    )(x, *ys, *ws)


def _gla_body(q_ref, k_ref, v_ref, og_ref, gr_ref, s0_ref, wg_ref, bg_ref, gn_ref, *rest, aliased, L):
    o_ref, s_out_ref, s_ref, b_scr = rest[1:] if aliased else rest
    step = pl.program_id(1)
    n_sub = q_ref.shape[0] // L

    @pl.when(step == 0)
    def _():
        for h in range(GLA_HEADS):
            s_ref[h] = s0_ref[0, h].T

    pre = _dot(gr_ref[...].astype(BF16), wg_ref[...]) + bg_ref[...]
    log_a = (jnp.minimum(pre, 0.0) - jnp.log1p(jnp.exp(-jnp.abs(pre)))) * (1.0 / GLA_TAU)
    causal = lax.broadcasted_iota(jnp.int32, (L, L), 0) >= lax.broadcasted_iota(jnp.int32, (L, L), 1)
    tri = jnp.where(causal, 1.0, 0.0).astype(BF16)
    gn = gn_ref[...]

    cums = [_dot_exact_rhs(tri, log_a[c * L:(c + 1) * L, :]) for c in range(n_sub)]
    total = cums[0][L - 1:L, :]
    for b_c in cums[1:]:
        total = jnp.minimum(total, b_c[L - 1:L, :])
    mild = jnp.min(total) >= -GLA_SAFE_RANGE

    def intra_factored(c, h, q, b):
        r0 = c * L
        b_mid = b[L // 2 - 1:L // 2, :]
        q_mid = (q * jnp.exp(b - b_mid)).astype(BF16)
        k_mid = (k_ref[r0:r0 + L, h * GLA_DK:(h + 1) * GLA_DK] * jnp.exp(b_mid - b)).astype(BF16)
        att = jnp.where(causal, _dot_nt(q_mid, k_mid), 0.0)
        return _dot(att.astype(BF16), v_ref[r0:r0 + L, h * GLA_DV:(h + 1) * GLA_DV].astype(BF16))

    def intra_pairwise(c, h, q, b):
        r0 = c * L
        b_scr[...] = b
        t_idx = lax.broadcasted_iota(jnp.int32, (L, 1), 0)

        def add_keys(g, o):
            s0 = pl.multiple_of(g * SUBLANE, SUBLANE)
            k_g = k_ref[pl.ds(r0 + s0, SUBLANE), h * GLA_DK:(h + 1) * GLA_DK]
            v_g = v_ref[pl.ds(r0 + s0, SUBLANE), h * GLA_DV:(h + 1) * GLA_DV]
            b_g = b_scr[pl.ds(s0, SUBLANE), :]
            for j in range(SUBLANE):
                decay = jnp.exp(jnp.minimum(b - b_g[j:j + 1, :], 0.0))
                w = jnp.sum(q * k_g[j:j + 1, :] * decay, axis=-1, keepdims=True)
                o = o + jnp.where(t_idx >= s0 + j, w, 0.0) * v_g[j:j + 1, :]
            return o

        return lax.fori_loop(0, L // SUBLANE, add_keys, jnp.zeros((L, GLA_DV), F32))

    def run(intra):
        states = [s_ref[h] for h in range(GLA_HEADS)]
        for c in range(n_sub):
            r0 = c * L
            for h in range(GLA_HEADS):
                kc = slice(h * GLA_DK, (h + 1) * GLA_DK)
                vc = slice(h * GLA_DV, (h + 1) * GLA_DV)
                b = cums[c][:, kc]
                b_last = b[L - 1:L, :]
                q = q_ref[r0:r0 + L, kc] * (GLA_DK ** -0.5)
                q_in = (q * jnp.exp(b)).astype(BF16)
                k_end = (k_ref[r0:r0 + L, kc] * jnp.exp(b_last - b)).astype(BF16)
                s_old = states[h]
                o = intra(c, h, q, b) + _dot_nt(q_in, s_old.astype(BF16))
                states[h] = s_old * jnp.exp(b_last) + _dot_tn(v_ref[r0:r0 + L, vc].astype(BF16), k_end)
                o_ref[r0:r0 + L, vc] = (_rms(o, gn) * _silu(og_ref[r0:r0 + L, vc])).astype(o_ref.dtype)

        for h in range(GLA_HEADS):
            s_ref[h] = states[h]

        @pl.when(step == pl.num_programs(1) - 1)
        def _():
            for h in range(GLA_HEADS):
                s_out_ref[0, h] = states[h].T

    @pl.when(mild)
    def _():
        run(intra_factored)

    @pl.when(jnp.logical_not(mild))
    def _():
        run(intra_pairwise)


def _gla(proj, row0, n_seq, seq_len, chunk, n_sub, s0, w_gate, b_gate, g_norm, dst=None):
    rows = proj.shape[0]
    tr = chunk * n_sub
    n_steps = seq_len // tr
    base = row0 // tr

    def rmap(col_block):
        return lambda b, s: (base + b * n_steps + s, col_block)

    const = lambda b, s: (0, 0)
    in_specs = [
        pl.BlockSpec((tr, GLA_QK), rmap(AB_Q // GLA_QK)),
        pl.BlockSpec((tr, GLA_QK), rmap(AB_K // GLA_QK)),
        pl.BlockSpec((tr, GLA_VW), rmap(AB_V // GLA_VW)),
        pl.BlockSpec((tr, GLA_VW), rmap(AB_OG // GLA_VW)),
        pl.BlockSpec((tr, LANE), rmap(AB_GR // LANE)),
        pl.BlockSpec((1, GLA_HEADS, GLA_DK, GLA_DV), lambda b, s: (b, 0, 0, 0)),
        pl.BlockSpec((LANE, GLA_QK), const),
        pl.BlockSpec((1, GLA_QK), const),
        pl.BlockSpec((1, GLA_DV), const),
    ]
    args = [proj, proj, proj, proj, proj, s0, w_gate, b_gate, g_norm]
    aliases = {}
    if dst is not None:
        in_specs.append(pl.BlockSpec(memory_space=pl.ANY))
        args.append(dst)
        aliases = {len(args) - 1: 0}
    return pl.pallas_call(
        functools.partial(_gla_body, aliased=dst is not None, L=chunk),
        grid=(n_seq, n_steps),
        in_specs=in_specs,
        out_specs=[
            pl.BlockSpec((tr, GLA_VW), rmap(0)),
            pl.BlockSpec((1, GLA_HEADS, GLA_DK, GLA_DV), lambda b, s: (b, 0, 0, 0)),
        ],
        out_shape=[
            jax.ShapeDtypeStruct((rows, GLA_VW), BF16),
            jax.ShapeDtypeStruct((n_seq, GLA_HEADS, GLA_DK, GLA_DV), F32),
        ],
        scratch_shapes=[pltpu.VMEM((GLA_HEADS, GLA_DV, GLA_DK), F32), pltpu.VMEM((chunk, GLA_DK), F32)],
        input_output_aliases=aliases,
        compiler_params=_params("parallel", "arbitrary"),
        name="gla",
    )(*args)


def _rope128(x, cos_t, sin_t):
    return x * cos_t + pltpu.roll(x, MLA_ROPE, 1) * sin_t


def _mla_pre_body(cq_ref, ckv_ref, kpe_ref, cos_ref, sin_ref, gq_ref, gkv_ref, wuq_ref, wuk_ref, wuvt_ref,
                  q_ref, k_ref, vt_ref, ckvn_ref, kper_ref):
    scale = MLA_SCORE_SCALE
    cos_t = cos_ref[...]
    sin_t = sin_ref[...]
    qf = _dot(_rms(cq_ref[...], gq_ref[...]).astype(BF16), wuq_ref[...])
    ckv_n = _rms(ckv_ref[...], gkv_ref[...])
    ckvn_ref[...] = ckv_n
    kpe_r = _rope128(kpe_ref[...], cos_t, sin_t)
    kper_ref[...] = kpe_r
    kpe_b = kpe_r.astype(BF16)
    ckv_b = ckv_n.astype(BF16)
    k_nope = _dot(ckv_b, wuk_ref[...])
    vt_ref[...] = _dot_nt(wuvt_ref[...], ckv_b).astype(BF16)
    for h in range(MLA_HEADS):
        o = h * MLA_QKW
        q_ref[:, o:o + MLA_NOPE] = (qf[:, o:o + MLA_NOPE] * scale).astype(BF16)
        q_ref[:, o + MLA_NOPE:o + MLA_QKW] = (_rope128(qf[:, o + MLA_NOPE:o + MLA_QKW], cos_t, sin_t) * scale).astype(BF16)
        k_ref[:, o:o + MLA_NOPE] = k_nope[:, h * MLA_NOPE:(h + 1) * MLA_NOPE].astype(BF16)
        k_ref[:, o + MLA_NOPE:o + MLA_QKW] = kpe_b


def _mla_pre(proj, cos_t, sin_t, g_q, g_kv, w_uq, w_uk, w_uv_t):
    rows = proj.shape[0]
    tm = _tile(rows, 256, LANE)
    qkw = MLA_HEADS * MLA_QKW
    vw = MLA_HEADS * MLA_V
    row = lambda i: (i, 0)
    const = lambda i: (0, 0)
    return pl.pallas_call(
        _mla_pre_body,
        grid=(rows // tm,),
        in_specs=[
            pl.BlockSpec((tm, MLA_Q_LORA), lambda i: (i, AB_CQ // MLA_Q_LORA)),
            pl.BlockSpec((tm, MLA_KV_LORA), lambda i: (i, AB_CKV // MLA_KV_LORA)),
            pl.BlockSpec((tm, LANE), lambda i: (i, AB_KPE // LANE)),
            pl.BlockSpec((tm, LANE), row),
            pl.BlockSpec((tm, LANE), row),
            pl.BlockSpec((1, MLA_Q_LORA), const),
            pl.BlockSpec((1, MLA_KV_LORA), const),
            pl.BlockSpec((MLA_Q_LORA, qkw), const),
            pl.BlockSpec((MLA_KV_LORA, MLA_HEADS * MLA_NOPE), const),
            pl.BlockSpec((vw, MLA_KV_LORA), const),
        ],
        out_specs=[
            pl.BlockSpec((tm, qkw), row),
            pl.BlockSpec((tm, qkw), row),
            pl.BlockSpec((vw, tm), lambda i: (0, i)),
            pl.BlockSpec((tm, MLA_KV_LORA), row),
            pl.BlockSpec((tm, LANE), row),
        ],
        out_shape=[
            jax.ShapeDtypeStruct((rows, qkw), BF16),
            jax.ShapeDtypeStruct((rows, qkw), BF16),
            jax.ShapeDtypeStruct((vw, rows), BF16),
            jax.ShapeDtypeStruct((rows, MLA_KV_LORA), F32),
            jax.ShapeDtypeStruct((rows, LANE), F32),
        ],
        compiler_params=_params("parallel"),
        name="mla_pre",
    )(proj, proj, proj, cos_t, sin_t, g_q, g_kv, w_uq, w_uk, w_uv_t)


def _flash_body(q_ref, k_ref, vt_ref, dst_ref, o_ref, m_ref, l_ref, acc_ref, *, blk):
    del dst_ref
    qi = pl.program_id(1)
    n_heads = q_ref.shape[1] // MLA_QKW
    m_ref[...] = jnp.full(m_ref.shape, -jnp.inf, F32)
    l_ref[...] = jnp.zeros(l_ref.shape, F32)
    acc_ref[...] = jnp.zeros(acc_ref.shape, F32)

    def block(ki, masked):
        start = pl.multiple_of(ki * blk, blk)
        for h in range(n_heads):
            qk = slice(h * MLA_QKW, (h + 1) * MLA_QKW)
            st = _dot_nt(k_ref[pl.ds(start, blk), qk], q_ref[:, qk])
            if masked:
                k_chunk = lax.broadcasted_iota(jnp.int32, (blk, blk), 0) // CHUNK
                q_chunk = lax.broadcasted_iota(jnp.int32, (blk, blk), 1) // CHUNK
                st = jnp.where(k_chunk <= q_chunk, st, -jnp.inf)
            m_old = m_ref[h]
            m_new = jnp.maximum(m_old, jnp.max(st, axis=0, keepdims=True))
            alpha = jnp.exp2(m_old - m_new)
            p = jnp.exp2(st - m_new)
            l_ref[h] = alpha * l_ref[h] + jnp.sum(p, axis=0, keepdims=True)
            vt = vt_ref[h * MLA_V:(h + 1) * MLA_V, pl.ds(start, blk)]
            acc_ref[h] = alpha * acc_ref[h] + _dot(vt, p.astype(BF16))
            m_ref[h] = m_new

    def interior(ki, carry):
        block(ki, False)
        return carry

    lax.fori_loop(0, qi, interior, 0)
    block(qi, True)
    for h in range(n_heads):
        o_ref[:, h * MLA_V:(h + 1) * MLA_V] = (acc_ref[h] / l_ref[h]).T.astype(o_ref.dtype)


def _flash(q, k, vt, seq_len, tile, dst):
    blk = _tile(seq_len, tile, LANE)
    assert blk % CHUNK == 0
    hp = MLA_HEADS_PER_STEP
    return pl.pallas_call(
        functools.partial(_flash_body, blk=blk),
        grid=(MLA_HEADS // hp, seq_len // blk),
        in_specs=[
            pl.BlockSpec((blk, hp * MLA_QKW), lambda h, qi: (qi, h)),
            pl.BlockSpec((seq_len, hp * MLA_QKW), lambda h, qi: (0, h)),
            pl.BlockSpec((hp * MLA_V, seq_len), lambda h, qi: (h, 0)),
            pl.BlockSpec(memory_space=pl.ANY),
        ],
        out_specs=pl.BlockSpec((blk, hp * MLA_V), lambda h, qi: (qi, h)),
        out_shape=jax.ShapeDtypeStruct(dst.shape, dst.dtype),
        scratch_shapes=[pltpu.VMEM((hp, 1, blk), F32), pltpu.VMEM((hp, 1, blk), F32),
                        pltpu.VMEM((hp, MLA_V, blk), F32)],
        input_output_aliases={3: 0},
        compiler_params=_params("parallel", "arbitrary"),
        name="mla_flash",
    )(q, k, vt, dst)


def _mla_hist_body(q_ref, kn_ref, ckvn_ref, ckv_ref, kpe_ref, wuk_ref, wuv_ref, dst_ref, o_ref, kpe_pad_ref, *, past_len):
    del dst_ref
    t_new = q_ref.shape[0]
    v_new = _dot(ckvn_ref[...].astype(BF16), wuv_ref[...]).astype(BF16)
    ckv_b = ckv_ref[0].astype(BF16)
    k_past = _dot(ckv_b, wuk_ref[...]).astype(BF16)
    v_past = _dot(ckv_b, wuv_ref[...]).astype(BF16)
    kpe_pad_ref[:, :MLA_ROPE] = kpe_ref[0].astype(BF16)
    kpe_pad_ref[:, MLA_ROPE:] = jnp.zeros((past_len, LANE - MLA_ROPE), BF16)
    kpe_past = kpe_pad_ref[...]

    q_chunk = (past_len + lax.broadcasted_iota(jnp.int32, (t_new, past_len), 0)) // CHUNK
    mask_past = lax.broadcasted_iota(jnp.int32, (t_new, past_len), 1) // CHUNK <= q_chunk
    q_chunk_n = (past_len + lax.broadcasted_iota(jnp.int32, (t_new, t_new), 0)) // CHUNK
    mask_new = (past_len + lax.broadcasted_iota(jnp.int32, (t_new, t_new), 1)) // CHUNK <= q_chunk_n

    for h in range(MLA_HEADS):
        o = h * MLA_QKW
        q_nope = q_ref[:, o:o + MLA_NOPE]
        q_pe = q_ref[:, o + MLA_NOPE:o + MLA_QKW]
        s_past = _dot_nt(q_nope, k_past[:, h * MLA_NOPE:(h + 1) * MLA_NOPE]) + _dot_nt(q_pe, kpe_past)
        s_new = _dot_nt(q_ref[:, o:o + MLA_QKW], kn_ref[:, o:o + MLA_QKW])
        s_past = jnp.where(mask_past, s_past, -jnp.inf)
        s_new = jnp.where(mask_new, s_new, -jnp.inf)
        m = jnp.maximum(jnp.max(s_past, axis=-1, keepdims=True), jnp.max(s_new, axis=-1, keepdims=True))
        p_past = jnp.exp2(s_past - m)
        p_new = jnp.exp2(s_new - m)
        denom = jnp.sum(p_past, axis=-1, keepdims=True) + jnp.sum(p_new, axis=-1, keepdims=True)
        pv = _dot(p_past.astype(BF16), v_past[:, h * MLA_V:(h + 1) * MLA_V])
        pv = pv + _dot(p_new.astype(BF16), v_new[:, h * MLA_V:(h + 1) * MLA_V])
        o_ref[:, h * MLA_V:(h + 1) * MLA_V] = (pv / denom).astype(o_ref.dtype)


def _mla_hist(q, k, ckv_n, row0, n_seq, t_new, ckv_past, kpe_past, w_uk, w_uv, dst):
    past_len = ckv_past.shape[1]
    base = row0 // t_new
    qkw = MLA_HEADS * MLA_QKW
    vw = MLA_HEADS * MLA_V
    row = lambda b: (base + b, 0)
    const = lambda b: (0, 0)
    return pl.pallas_call(
        functools.partial(_mla_hist_body, past_len=past_len),
        grid=(n_seq,),
        in_specs=[
            pl.BlockSpec((t_new, qkw), row),
            pl.BlockSpec((t_new, qkw), row),
            pl.BlockSpec((t_new, MLA_KV_LORA), row),
            pl.BlockSpec((1, past_len, MLA_KV_LORA), lambda b: (b, 0, 0)),
            pl.BlockSpec((1, past_len, MLA_ROPE), lambda b: (b, 0, 0)),
            pl.BlockSpec((MLA_KV_LORA, MLA_HEADS * MLA_NOPE), const),
            pl.BlockSpec((MLA_KV_LORA, vw), const),
            pl.BlockSpec(memory_space=pl.ANY),
        ],
        out_specs=pl.BlockSpec((t_new, vw), row),
        out_shape=jax.ShapeDtypeStruct(dst.shape, dst.dtype),
        scratch_shapes=[pltpu.VMEM((past_len, LANE), BF16)],
        input_output_aliases={7: 0},
        compiler_params=_params("parallel"),
        name="mla_hist",
    )(q, k, ckv_n, ckv_past, kpe_past, w_uk, w_uv, dst)


SSD_L = 128
SSD_GROUPS_PER_STEP = 2


def _conv_silu(x_ref, buf_ref, w_ref, b_ref, ext_ref, first):
    L = x_ref.shape[0]

    @pl.when(first)
    def _():
        ext_ref[8 - (SSD_CONV - 1):8, :] = buf_ref[0]

    @pl.when(jnp.logical_not(first))
    def _():
        ext_ref[0:8, :] = ext_ref[L:L + 8, :]

    ext_ref[8:8 + L, :] = x_ref[...]
    y = b_ref[...]
    for j in range(SSD_CONV):
        y = y + ext_ref[8 - (SSD_CONV - 1) + j:8 - (SSD_CONV - 1) + j + L, :] * w_ref[j:j + 1, :]
    return _silu(y)


def _pad_rows(x, n):
    if x.shape[0] == n:
        return x
    return jnp.concatenate([x, jnp.zeros((n - x.shape[0], x.shape[1]), x.dtype)], axis=0)


def _ssd_body(z_ref, x_ref, bm_ref, cm_ref, dt_ref, bufx_ref, bufb_ref, bufc_ref, wx_ref, wb_ref, wc_ref,
              bx_ref, bb_ref, bc_ref, dtb_ref, alog_ref, dsk_ref, gn_ref, s0_ref, dst_ref,
              y_ref, s_out_ref, s_ref, extx_ref, extb_ref, extc_ref):
    del dst_ref
    c = pl.program_id(2)
    first = c == 0
    L = x_ref.shape[0]
    LS = SSD_L
    n_grp = x_ref.shape[1] // SSD_GW

    @pl.when(first)
    def _():
        for p in range(n_grp):
            s_ref[p] = s0_ref[0, p * SSD_REP:(p + 1) * SSD_REP].reshape(SSD_GW, SSD_N).T

    xs_all = _conv_silu(x_ref, bufx_ref, wx_ref, bx_ref, extx_ref, first)
    bm_all = _conv_silu(bm_ref, bufb_ref, wb_ref, bb_ref, extb_ref, first)
    cm_all = _conv_silu(cm_ref, bufc_ref, wc_ref, bc_ref, extc_ref, first)

    row = lax.broadcasted_iota(jnp.int32, (LS, LS), 0)
    col = lax.broadcasted_iota(jnp.int32, (LS, LS), 1)
    tri = jnp.where(row >= col, 1.0, 0.0).astype(BF16)
    erow = lax.broadcasted_iota(jnp.int32, (LANE, SSD_GW), 0)
    ecol = lax.broadcasted_iota(jnp.int32, (LANE, SSD_GW), 1) // SSD_HEADDIM
    expand = jnp.where(erow == ecol, 1.0, 0.0).astype(BF16)
    causal = lax.broadcasted_iota(jnp.int32, (L, LS), 0) >= lax.broadcasted_iota(jnp.int32, (L, LS), 1)
    lane_half = lax.broadcasted_iota(jnp.int32, (LS, LANE), 1) < SSD_HEADDIM

    new_states = []
    for p in range(n_grp):
        gw = slice(p * SSD_GW, (p + 1) * SSD_GW)
        gn = slice(p * SSD_N, (p + 1) * SSD_N)
        xs = xs_all[:, gw]
        cm_b = cm_all[:, gn].astype(BF16)
        dt = _softplus(dt_ref[:, p * LANE:(p + 1) * LANE] + dtb_ref[p])
        xs_p = _pad_rows(xs, LS)
        bm_p = _pad_rows(bm_all[:, gn], LS).astype(BF16)
        dt_p = _pad_rows(dt, LS)
        da = dt_p * (-jnp.exp(alog_ref[p]))

        cum = _dot_exact_rhs(tri, da)
        cum_t = cum.T
        dt_t = dt_p.T
        cum_q = cum[:L]
        last = cum[LS - 1:LS, :]
        e_cum = _dot_exact_lhs(jnp.exp(cum_q), expand)
        w_dec = _dot_exact_lhs(jnp.exp(last - cum) * dt_p, expand)
        e_last = _dot_exact_lhs(jnp.exp(last), expand)
        cb = _dot_nt(cm_b, bm_p)

        def head_weights(r):
            diff = cum_q[:, r:r + 1] - cum_t[r:r + 1, :]
            return (cb * jnp.exp(jnp.where(causal, diff, -jnp.inf)) * dt_t[r:r + 1, :]).astype(BF16)

        y_parts = []
        for q in range(SSD_REP // 2):
            w_pair = jnp.concatenate([head_weights(2 * q), head_weights(2 * q + 1)], axis=1)
            x_pair = xs_p[:, q * LANE:(q + 1) * LANE]
            x_bd = jnp.concatenate([jnp.where(lane_half, x_pair, 0.0), jnp.where(lane_half, 0.0, x_pair)], axis=0)
            y_parts.append(_dot(w_pair, x_bd.astype(BF16)))
        y = jnp.concatenate(y_parts, axis=1)

        s_old = s_ref[p]
        y = y + _dot(cm_b, s_old.astype(BF16)) * e_cum
        s_new = s_old * e_last + _dot_tn(bm_p, (xs_p * w_dec).astype(BF16))
        s_ref[p] = s_new
        new_states.append(s_new)

        y = y + xs * dsk_ref[:, gw]
        y = y * _silu(z_ref[:, gw])
        y_ref[:, gw] = _rms(y, gn_ref[:, gw]).astype(y_ref.dtype)

    @pl.when(c == pl.num_programs(2) - 1)
    def _():
        for p in range(n_grp):
            s_out_ref[0, p * SSD_REP:(p + 1) * SSD_REP] = new_states[p].T.reshape(SSD_REP, SSD_HEADDIM, SSD_N)


def _ssd(proj, row0, n_seq, seq_len, chunk, conv_buf, s0, w_conv, b_conv, dt_bias, a_log, d_skip, g_norm, dst):
    d_inner = SSD_GROUPS * SSD_GW
    L = chunk
    n_chunks = seq_len // L
    base = row0 // L
    bn = SSD_GROUPS * SSD_N
    gp = SSD_GROUPS_PER_STEP
    gw, gn, gl = gp * SSD_GW, gp * SSD_N, gp * LANE
    off_x, off_b, off_c, off_dt = d_inner, 2 * d_inner, 2 * d_inner + bn, 2 * d_inner + 2 * bn

    def rmap(col_blocks_off):
        return lambda b, g, c: (base + b * n_chunks + c, col_blocks_off + g)

    def cmap(col_blocks_off):
        return lambda b, g, c: (0, col_blocks_off + g)

    nb = SSD_CONV - 1
    in_specs = [
        pl.BlockSpec((L, gw), rmap(0)),
        pl.BlockSpec((L, gw), rmap(off_x // gw)),
        pl.BlockSpec((L, gn), rmap(off_b // gn)),
        pl.BlockSpec((L, gn), rmap(off_c // gn)),
        pl.BlockSpec((L, gl), rmap(off_dt // gl)),
        pl.BlockSpec((1, nb, gw), lambda b, g, c: (b, 0, g)),
        pl.BlockSpec((1, nb, gn), lambda b, g, c: (b, 0, d_inner // gn + g)),
        pl.BlockSpec((1, nb, gn), lambda b, g, c: (b, 0, (d_inner + bn) // gn + g)),
        pl.BlockSpec((SSD_CONV, gw), cmap(0)),
        pl.BlockSpec((SSD_CONV, gn), cmap(d_inner // gn)),
        pl.BlockSpec((SSD_CONV, gn), cmap((d_inner + bn) // gn)),
        pl.BlockSpec((1, gw), cmap(0)),
        pl.BlockSpec((1, gn), cmap(d_inner // gn)),
        pl.BlockSpec((1, gn), cmap((d_inner + bn) // gn)),
        pl.BlockSpec((gp, 1, LANE), lambda b, g, c: (g, 0, 0)),
        pl.BlockSpec((gp, 1, LANE), lambda b, g, c: (g, 0, 0)),
        pl.BlockSpec((1, gw), cmap(0)),
        pl.BlockSpec((1, gw), cmap(0)),
        pl.BlockSpec((1, gp * SSD_REP, SSD_HEADDIM, SSD_N), lambda b, g, c: (b, g, 0, 0)),
        pl.BlockSpec(memory_space=pl.ANY),
    ]
    args = [proj] * 5 + [conv_buf] * 3 + [w_conv] * 3 + [b_conv] * 3 + [dt_bias, a_log, d_skip, g_norm, s0, dst]
    return pl.pallas_call(
        _ssd_body,
        grid=(n_seq, SSD_GROUPS // gp, n_chunks),
        in_specs=in_specs,
        out_specs=[
            pl.BlockSpec((L, gw), rmap(0)),
            pl.BlockSpec((1, gp * SSD_REP, SSD_HEADDIM, SSD_N), lambda b, g, c: (b, g, 0, 0)),
        ],
        out_shape=[
            jax.ShapeDtypeStruct(dst.shape, dst.dtype),
            jax.ShapeDtypeStruct((n_seq, SSD_GROUPS * SSD_REP, SSD_HEADDIM, SSD_N), F32),
        ],
        scratch_shapes=[
            pltpu.VMEM((gp, SSD_N, SSD_GW), F32),
            pltpu.VMEM((L + 8, gw), F32),
            pltpu.VMEM((L + 8, gn), F32),
            pltpu.VMEM((L + 8, gn), F32),
        ],
        input_output_aliases={len(args) - 1: 0},
        compiler_params=_params("parallel", "parallel", "arbitrary"),
        name="ssd",
    )(*args)


def _rot_cols(w):
    half = w.shape[-1] // 2
    return jnp.concatenate([-w[..., half:], w[..., :half]], axis=-1)


def _prep_ab_in(w):
    d = w.shape[0]
    o = 0
    parts = {}
    for name, size in (("q", GLA_QK), ("k", GLA_QK), ("v", GLA_VW), ("gr", GLA_GATE_RANK), ("og", GLA_VW),
                       ("cq", MLA_Q_LORA), ("ckv", MLA_KV_LORA), ("kpe", MLA_ROPE)):
        parts[name] = w[:, o:o + size]
        o += size
    zeros = jnp.zeros((d, LANE - GLA_GATE_RANK), w.dtype)
    out = jnp.concatenate([parts["q"], parts["k"], parts["v"], parts["og"], parts["cq"], parts["ckv"],
                           parts["kpe"], _rot_cols(parts["kpe"]), parts["gr"], zeros], axis=1)
    assert out.shape[1] == AB_W
    return out.astype(BF16)


def _prep_uq(w):
    w = w.reshape(w.shape[0], MLA_HEADS, MLA_NOPE + MLA_ROPE)
    pe = w[..., MLA_NOPE:]
    out = jnp.concatenate([w[..., :MLA_NOPE], pe, _rot_cols(pe)], axis=-1)
    return out.reshape(w.shape[0], MLA_HEADS * MLA_QKW).astype(BF16)


def _prep_ssd_in(w, d_inner):
    d = w.shape[0]
    n_heads = SSD_GROUPS * SSD_REP
    main = w[:, :2 * d_inner + 2 * SSD_GROUPS * SSD_N]
    dt = w[:, 2 * d_inner + 2 * SSD_GROUPS * SSD_N:].reshape(d, SSD_GROUPS, SSD_REP)
    assert dt.shape[1] * dt.shape[2] == n_heads
    dt = jnp.concatenate([dt, jnp.zeros((d, SSD_GROUPS, LANE - SSD_REP), w.dtype)], axis=-1)
    return jnp.concatenate([main, dt.reshape(d, SSD_GROUPS * LANE)], axis=1).astype(BF16)


def _head_table(v):
    v = v.reshape(SSD_GROUPS, 1, SSD_REP)
    return jnp.concatenate([v, jnp.zeros((SSD_GROUPS, 1, LANE - SSD_REP), v.dtype)], axis=-1)


def _rope_tables(pos):
    half = MLA_ROPE // 2
    inv = ROPE_THETA ** (-jnp.arange(half, dtype=F32) / half)
    ang = pos.astype(F32)[:, None] * inv[None, :]
    zeros = jnp.zeros((pos.shape[0], LANE - MLA_ROPE), F32)
    cos = jnp.cos(ang)
    sin = jnp.sin(ang)
    return jnp.concatenate([cos, cos, zeros], axis=1), jnp.concatenate([sin, sin, zeros], axis=1)


def kernel(x_prompt, x_sample, cache_mla_ckv, cache_mla_kpe, state_gla, state_ssd, state_ssd_conv, g_ffn1, w_ffn1_gate, w_ffn1_up, w_ffn1_down, g_mix, g_ffn2, w_ffn2_gate, w_ffn2_up, w_ffn2_down, w_ab_in, w_gla_gate_up, b_gla_gate, g_gla_norm, g_mla_q_norm, w_mla_uq, g_mla_kv_norm, w_mla_uk, w_mla_uv, w_ab_out, w_ssd_in, w_ssd_conv, b_ssd_conv, ssd_dt_bias, ssd_a_log, ssd_d, g_ssd_norm, w_ssd_out, g_final):
    n_p, t_p, d = x_prompt.shape
    n_s, t_s, _ = x_sample.shape
    assert n_p == 1
    depth = g_ffn1.shape[0]
    past_len = cache_mla_ckv.shape[2]
    rows_p = n_p * t_p
    d_inner = w_ssd_out.shape[1]
    assert d_inner == SSD_GROUPS * SSD_GW

    x = jnp.concatenate([x_prompt.reshape(rows_p, d), x_sample.reshape(n_s * t_s, d)], axis=0)
    rows = rows_p + n_s * t_s
    pos = jnp.concatenate([jnp.arange(t_p, dtype=jnp.int32),
                           jnp.tile(past_len + jnp.arange(t_s, dtype=jnp.int32), n_s)])
    cos_t, sin_t = _rope_tables(pos)
    row2 = lambda v: v.reshape(1, -1)

    outs = {k: [] for k in ("p_ckv", "p_kpe", "p_gla", "p_ssd", "p_conv", "s_ckv", "s_kpe", "s_gla", "s_ssd", "s_conv")}
    for layer in range(depth):
        i = layer // 2
        x = _ffn(x, row2(g_ffn1[layer]), w_ffn1_gate, w_ffn1_up, w_ffn1_down, layer)
        if layer % 2 == 0:
            proj = _norm_proj(x, row2(g_mix[layer]), _prep_ab_in(w_ab_in[i]), AB_W // 2)
            w_gate = jnp.concatenate([w_gla_gate_up[i], jnp.zeros((LANE - GLA_GATE_RANK, GLA_QK), F32)], axis=0).astype(BF16)
            gla_args = (w_gate, row2(b_gla_gate[i]), row2(g_gla_norm[i]))
            gla_chunk = min(CHUNK, t_p)
            o_gla, p_gla = _gla(proj, 0, n_p, t_p, gla_chunk, min(4, t_p // gla_chunk),
                                jnp.zeros((n_p, GLA_HEADS, GLA_DK, GLA_DV), F32), *gla_args,
                                dst=jnp.zeros((rows, GLA_VW), BF16))
            o_gla, s_gla = _gla(proj, rows_p, n_s, t_s, t_s, 1, state_gla[i], *gla_args, dst=o_gla)

            w_uk = w_mla_uk[i].astype(BF16)
            w_uv = w_mla_uv[i].astype(BF16)
            q, k, vt, ckv_n, kpe_r = _mla_pre(proj, cos_t, sin_t, row2(g_mla_q_norm[i]), row2(g_mla_kv_norm[i]),
                                              _prep_uq(w_mla_uq[i]), w_uk, w_uv.T)
            o_mla = _flash(q, k, vt, t_p, 512, jnp.zeros((rows, MLA_HEADS * MLA_V), BF16))
            o_mla = _mla_hist(q, k, ckv_n, rows_p, n_s, t_s, cache_mla_ckv[i], cache_mla_kpe[i], w_uk, w_uv, o_mla)

            w_out = w_ab_out[i].astype(BF16)
            x = _out_proj(x, [o_gla, o_mla], [w_out[:GLA_VW], w_out[GLA_VW:]])
            outs["p_ckv"].append(ckv_n[:rows_p].reshape(n_p, t_p, MLA_KV_LORA))
            outs["s_ckv"].append(ckv_n[rows_p:].reshape(n_s, t_s, MLA_KV_LORA))
            outs["p_kpe"].append(kpe_r[:rows_p, :MLA_ROPE].reshape(n_p, t_p, MLA_ROPE))
            outs["s_kpe"].append(kpe_r[rows_p:, :MLA_ROPE].reshape(n_s, t_s, MLA_ROPE))
            outs["p_gla"].append(p_gla)
            outs["s_gla"].append(s_gla)
        else:
            n_heads = SSD_GROUPS * SSD_REP
            conv_ch = d_inner + 2 * SSD_GROUPS * SSD_N
            proj = _norm_proj(x, row2(g_mix[layer]), _prep_ssd_in(w_ssd_in[i], d_inner),
                              (2 * d_inner + 2 * SSD_GROUPS * SSD_N + SSD_GROUPS * LANE) // 8)
            ssd_args = (w_ssd_conv[i], row2(b_ssd_conv[i]), _head_table(ssd_dt_bias[i]), _head_table(ssd_a_log[i]),
                        row2(jnp.repeat(ssd_d[i], SSD_HEADDIM)), row2(g_ssd_norm[i]))
            y, p_ssd = _ssd(proj, 0, n_p, t_p, min(SSD_L, t_p), jnp.zeros((n_p, SSD_CONV - 1, conv_ch), F32),
                            jnp.zeros((n_p, n_heads, SSD_HEADDIM, SSD_N), F32), *ssd_args,
                            dst=jnp.zeros((rows, d_inner), BF16))
            y, s_ssd = _ssd(proj, rows_p, n_s, t_s, t_s, state_ssd_conv[i], state_ssd[i], *ssd_args, dst=y)
            x = _out_proj(x, [y], [w_ssd_out[i].astype(BF16)])
            nb = SSD_CONV - 1
            outs["p_conv"].append(jnp.stack([
                lax.slice(proj, (b * t_p + t_p - nb, d_inner), (b * t_p + t_p, d_inner + conv_ch)) for b in range(n_p)]))
            outs["s_conv"].append(jnp.stack([
                lax.slice(proj, (rows_p + b * t_s + t_s - nb, d_inner), (rows_p + b * t_s + t_s, d_inner + conv_ch))
                for b in range(n_s)]))
            outs["p_ssd"].append(p_ssd)
            outs["s_ssd"].append(s_ssd)
        x = _ffn(x, row2(g_ffn2[layer]), w_ffn2_gate, w_ffn2_up, w_ffn2_down, layer,
                 g_final=row2(g_final) if layer == depth - 1 else None)

    y_prompt = x[:rows_p].reshape(n_p, t_p, d)
    y_sample = x[rows_p:].reshape(n_s, t_s, d)
    st = {k: jnp.stack(v) for k, v in outs.items()}
    return (y_prompt, y_sample, st["p_ckv"], st["p_kpe"], st["p_gla"], st["p_ssd"], st["p_conv"],
            st["s_ckv"], st["s_kpe"], st["s_gla"], st["s_ssd"], st["s_conv"])
```

```python
import functools

import jax
import jax.numpy as jnp
from jax import lax
from jax.experimental import pallas as pl
from jax.experimental.pallas import tpu as pltpu

F32 = jnp.float32
BF16 = jnp.bfloat16

RMS_EPS = 1e-6
CHUNK = 64

GLA_HEADS = 4
GLA_DK = 128
GLA_DV = 256
GLA_GATE_RANK = 16
GLA_TAU = 16.0
GLA_SAFE_RANGE = 40.0
GLA_QK = GLA_HEADS * GLA_DK
GLA_VW = GLA_HEADS * GLA_DV

MLA_HEADS = 8
MLA_Q_LORA = 512
MLA_KV_LORA = 512
MLA_NOPE = 128
MLA_ROPE = 64
MLA_V = 128
ROPE_THETA = 10000.0
MLA_QKW = 256
MLA_SCORE_SCALE = (MLA_NOPE + MLA_ROPE) ** -0.5 * 1.4426950408889634
MLA_HEADS_PER_STEP = 4

SSD_HEADDIM = 64
SSD_GROUPS = 8
SSD_N = 128
SSD_CONV = 4
SSD_GW = 512
SSD_REP = SSD_GW // SSD_HEADDIM

LANE = 128
SUBLANE = 8
AB_Q, AB_K, AB_V, AB_OG, AB_CQ, AB_CKV, AB_KPE, AB_GR, AB_W = 0, 512, 1024, 2048, 3072, 3584, 4096, 4224, 4352

VMEM_LIMIT = 56 * 1024 * 1024


def _tile(n, target, align):
    best = None
    for t in range(align, min(n, target) + 1, align):
        if n % t == 0:
            best = t
    return best if best is not None else n


def _rms(x, g):
    return x * lax.rsqrt(jnp.mean(x * x, axis=-1, keepdims=True) + RMS_EPS) * g


def _silu(x):
    return x * jax.nn.sigmoid(x)


def _softplus(x):
    return jnp.maximum(x, 0.0) + jnp.log1p(jnp.exp(-jnp.abs(x)))


def _dot(a, b):
    return jnp.dot(a, b, preferred_element_type=F32)


def _dot_nt(a, b):
    return lax.dot_general(a, b, (((1,), (1,)), ((), ())), preferred_element_type=F32)


def _dot_tn(a, b):
    return lax.dot_general(a, b, (((0,), (0,)), ((), ())), preferred_element_type=F32)


def _split3(x):
    h = x.astype(BF16)
    r = x - h.astype(F32)
    m = r.astype(BF16)
    l = (r - m.astype(F32)).astype(BF16)
    return h, m, l


def _dot_exact_rhs(a_bf16, x):
    h, m, l = _split3(x)
    return _dot(a_bf16, h) + _dot(a_bf16, m) + _dot(a_bf16, l)


def _dot_exact_lhs(x, b_bf16):
    h, m, l = _split3(x)
    return _dot(h, b_bf16) + _dot(m, b_bf16) + _dot(l, b_bf16)


def _params(*sem):
    return pltpu.CompilerParams(dimension_semantics=sem, vmem_limit_bytes=VMEM_LIMIT)


def _ffn_body(x_ref, g_ref, wg_ref, wu_ref, wd_ref, gf_ref, o_ref, h_ref, *, final_norm):
    j = pl.program_id(1)

    @pl.when(j == 0)
    def _():
        x = x_ref[...]
        h_ref[...] = _rms(x, g_ref[...]).astype(BF16)
        o_ref[...] = x

    h = h_ref[...]
    a = _dot(h, wg_ref[...].astype(BF16))
    b = _dot(h, wu_ref[...].astype(BF16))
    act = (_silu(a) * b) * 0.5
    o_ref[...] += _dot(act.astype(BF16), wd_ref[...].astype(BF16))

    if final_norm:
        @pl.when(j == pl.num_programs(1) - 1)
        def _():
            o_ref[...] = _rms(o_ref[...], gf_ref[...])


def _ffn(x, g, wg, wu, wd, layer, g_final=None):
    rows, d = x.shape
    f = wg.shape[2]
    tm = _tile(rows, 1056, 8)
    tf = _tile(f, 256, LANE)
    final = g_final is not None
    return pl.pallas_call(
        functools.partial(_ffn_body, final_norm=final),
        grid=(rows // tm, f // tf),
        in_specs=[
            pl.BlockSpec((tm, d), lambda i, j: (i, 0), pipeline_mode=pl.Buffered(1)),
            pl.BlockSpec((1, d), lambda i, j: (0, 0)),
            pl.BlockSpec((None, d, tf), lambda i, j: (layer, 0, j)),
            pl.BlockSpec((None, d, tf), lambda i, j: (layer, 0, j)),
            pl.BlockSpec((None, tf, d), lambda i, j: (layer, j, 0)),
            pl.BlockSpec((1, d), lambda i, j: (0, 0)),
        ],
        out_specs=pl.BlockSpec((tm, d), lambda i, j: (i, 0)),
        out_shape=jax.ShapeDtypeStruct((rows, d), F32),
        scratch_shapes=[pltpu.VMEM((tm, d), BF16)],
        compiler_params=_params("parallel", "arbitrary"),
        name="ffn",
    )(x, g, wg, wu, wd, g_final if final else g)


def _norm_proj_body(x_ref, g_ref, w_ref, o_ref, h_ref):
    @pl.when(pl.program_id(1) == 0)
    def _():
        h_ref[...] = _rms(x_ref[...], g_ref[...]).astype(BF16)

    o_ref[...] = _dot(h_ref[...], w_ref[...])


def _norm_proj(x, g, w, tn):
    rows, d = x.shape
    n = w.shape[1]
    tm = _tile(rows, 768, 8)
    return pl.pallas_call(
        _norm_proj_body,
        grid=(rows // tm, n // tn),
        in_specs=[
            pl.BlockSpec((tm, d), lambda i, j: (i, 0)),
            pl.BlockSpec((1, d), lambda i, j: (0, 0)),
            pl.BlockSpec((d, tn), lambda i, j: (0, j)),
        ],
        out_specs=pl.BlockSpec((tm, tn), lambda i, j: (i, j)),
        out_shape=jax.ShapeDtypeStruct((rows, n), F32),
        scratch_shapes=[pltpu.VMEM((tm, d), BF16)],
        compiler_params=_params("parallel", "arbitrary"),
        name="norm_proj",
    )(x, g, w)


def _out_proj_body(*refs, n_in):
    x_ref = refs[0]
    o_ref = refs[1 + 2 * n_in]
    acc = x_ref[...]
    for i in range(n_in):
        acc = acc + _dot(refs[1 + i][...], refs[1 + n_in + i][...])
    o_ref[...] = acc


def _out_proj(x, ys, ws):
    rows, d = x.shape
    n_in = len(ys)
    tm = _tile(rows, 768, 8)
    tn = _tile(d, 1024, LANE)
    in_specs = [pl.BlockSpec((tm, tn), lambda i, j: (i, j))]
    in_specs += [pl.BlockSpec((tm, y.shape[1]), lambda i, j: (i, 0)) for y in ys]
    in_specs += [pl.BlockSpec((w.shape[0], tn), lambda i, j: (0, j)) for w in ws]
    return pl.pallas_call(
        functools.partial(_out_proj_body, n_in=n_in),
        grid=(rows // tm, d // tn),
        in_specs=in_specs,
        out_specs=pl.BlockSpec((tm, tn), lambda i, j: (i, j)),
        out_shape=jax.ShapeDtypeStruct((rows, d), F32),
        compiler_params=_params("parallel", "arbitrary"),
        name="out_proj",
    )(x, *ys, *ws)


def _gla_body(q_ref, k_ref, v_ref, og_ref, gr_ref, s0_ref, wg_ref, bg_ref, gn_ref, *rest, aliased, L):
    o_ref, s_out_ref, s_ref, b_scr = rest[1:] if aliased else rest
    step = pl.program_id(1)
    n_sub = q_ref.shape[0] // L

    @pl.when(step == 0)
    def _():
        for h in range(GLA_HEADS):
            s_ref[h] = s0_ref[0, h].T

    pre = _dot(gr_ref[...].astype(BF16), wg_ref[...]) + bg_ref[...]
    log_a = (jnp.minimum(pre, 0.0) - jnp.log1p(jnp.exp(-jnp.abs(pre)))) * (1.0 / GLA_TAU)
    causal = lax.broadcasted_iota(jnp.int32, (L, L), 0) >= lax.broadcasted_iota(jnp.int32, (L, L), 1)
    tri = jnp.where(causal, 1.0, 0.0).astype(BF16)
    gn = gn_ref[...]

    cums = [_dot_exact_rhs(tri, log_a[c * L:(c + 1) * L, :]) for c in range(n_sub)]
    total = cums[0][L - 1:L, :]
    for b_c in cums[1:]:
        total = jnp.minimum(total, b_c[L - 1:L, :])
    mild = jnp.min(total) >= -GLA_SAFE_RANGE

    def intra_factored(c, h, q, b):
        r0 = c * L
        b_mid = b[L // 2 - 1:L // 2, :]
        q_mid = (q * jnp.exp(b - b_mid)).astype(BF16)
        k_mid = (k_ref[r0:r0 + L, h * GLA_DK:(h + 1) * GLA_DK] * jnp.exp(b_mid - b)).astype(BF16)
        att = jnp.where(causal, _dot_nt(q_mid, k_mid), 0.0)
        return _dot(att.astype(BF16), v_ref[r0:r0 + L, h * GLA_DV:(h + 1) * GLA_DV].astype(BF16))

    def intra_pairwise(c, h, q, b):
        r0 = c * L
        b_scr[...] = b
        t_idx = lax.broadcasted_iota(jnp.int32, (L, 1), 0)

        def add_keys(g, o):
            s0 = pl.multiple_of(g * SUBLANE, SUBLANE)
            k_g = k_ref[pl.ds(r0 + s0, SUBLANE), h * GLA_DK:(h + 1) * GLA_DK]
            v_g = v_ref[pl.ds(r0 + s0, SUBLANE), h * GLA_DV:(h + 1) * GLA_DV]
            b_g = b_scr[pl.ds(s0, SUBLANE), :]
            for j in range(SUBLANE):
                decay = jnp.exp(jnp.minimum(b - b_g[j:j + 1, :], 0.0))
                w = jnp.sum(q * k_g[j:j + 1, :] * decay, axis=-1, keepdims=True)
                o = o + jnp.where(t_idx >= s0 + j, w, 0.0) * v_g[j:j + 1, :]
            return o

        return lax.fori_loop(0, L // SUBLANE, add_keys, jnp.zeros((L, GLA_DV), F32))

    def run(intra):
        states = [s_ref[h] for h in range(GLA_HEADS)]
        for c in range(n_sub):
            r0 = c * L
            for h in range(GLA_HEADS):
                kc = slice(h * GLA_DK, (h + 1) * GLA_DK)
                vc = slice(h * GLA_DV, (h + 1) * GLA_DV)
                b = cums[c][:, kc]
                b_last = b[L - 1:L, :]
                q = q_ref[r0:r0 + L, kc] * (GLA_DK ** -0.5)
                q_in = (q * jnp.exp(b)).astype(BF16)
                k_end = (k_ref[r0:r0 + L, kc] * jnp.exp(b_last - b)).astype(BF16)
                s_old = states[h]
                o = intra(c, h, q, b) + _dot_nt(q_in, s_old.astype(BF16))
                states[h] = s_old * jnp.exp(b_last) + _dot_tn(v_ref[r0:r0 + L, vc].astype(BF16), k_end)
                o_ref[r0:r0 + L, vc] = (_rms(o, gn) * _silu(og_ref[r0:r0 + L, vc])).astype(o_ref.dtype)

        for h in range(GLA_HEADS):
            s_ref[h] = states[h]

        @pl.when(step == pl.num_programs(1) - 1)
        def _():
            for h in range(GLA_HEADS):
                s_out_ref[0, h] = states[h].T

    @pl.when(mild)
    def _():
        run(intra_factored)

    @pl.when(jnp.logical_not(mild))
    def _():
        run(intra_pairwise)


def _gla(proj, row0, n_seq, seq_len, chunk, n_sub, s0, w_gate, b_gate, g_norm, dst=None):
    rows = proj.shape[0]
    tr = chunk * n_sub
    n_steps = seq_len // tr
    base = row0 // tr

    def rmap(col_block):
        return lambda b, s: (base + b * n_steps + s, col_block)

    const = lambda b, s: (0, 0)
    in_specs = [
        pl.BlockSpec((tr, GLA_QK), rmap(AB_Q // GLA_QK)),
        pl.BlockSpec((tr, GLA_QK), rmap(AB_K // GLA_QK)),
        pl.BlockSpec((tr, GLA_VW), rmap(AB_V // GLA_VW)),
        pl.BlockSpec((tr, GLA_VW), rmap(AB_OG // GLA_VW)),
        pl.BlockSpec((tr, LANE), rmap(AB_GR // LANE)),
        pl.BlockSpec((1, GLA_HEADS, GLA_DK, GLA_DV), lambda b, s: (b, 0, 0, 0)),
        pl.BlockSpec((LANE, GLA_QK), const),
        pl.BlockSpec((1, GLA_QK), const),
        pl.BlockSpec((1, GLA_DV), const),
    ]
    args = [proj, proj, proj, proj, proj, s0, w_gate, b_gate, g_norm]
    aliases = {}
    if dst is not None:
        in_specs.append(pl.BlockSpec(memory_space=pl.ANY))
        args.append(dst)
        aliases = {len(args) - 1: 0}
    return pl.pallas_call(
        functools.partial(_gla_body, aliased=dst is not None, L=chunk),
        grid=(n_seq, n_steps),
        in_specs=in_specs,
        out_specs=[
            pl.BlockSpec((tr, GLA_VW), rmap(0)),
            pl.BlockSpec((1, GLA_HEADS, GLA_DK, GLA_DV), lambda b, s: (b, 0, 0, 0)),
        ],
        out_shape=[
            jax.ShapeDtypeStruct((rows, GLA_VW), BF16),
            jax.ShapeDtypeStruct((n_seq, GLA_HEADS, GLA_DK, GLA_DV), F32),
        ],
        scratch_shapes=[pltpu.VMEM((GLA_HEADS, GLA_DV, GLA_DK), F32), pltpu.VMEM((chunk, GLA_DK), F32)],
        input_output_aliases=aliases,
        compiler_params=_params("parallel", "arbitrary"),
        name="gla",
    )(*args)


def _rope128(x, cos_t, sin_t):
    return x * cos_t + pltpu.roll(x, MLA_ROPE, 1) * sin_t


def _mla_pre_body(cq_ref, ckv_ref, kpe_ref, cos_ref, sin_ref, gq_ref, gkv_ref, wuq_ref, wuk_ref, wuvt_ref,
                  q_ref, k_ref, vt_ref, ckvn_ref, kper_ref):
    scale = MLA_SCORE_SCALE
    cos_t = cos_ref[...]
    sin_t = sin_ref[...]
    qf = _dot(_rms(cq_ref[...], gq_ref[...]).astype(BF16), wuq_ref[...])
    ckv_n = _rms(ckv_ref[...], gkv_ref[...])
    ckvn_ref[...] = ckv_n
    kpe_r = _rope128(kpe_ref[...], cos_t, sin_t)
    kper_ref[...] = kpe_r
    kpe_b = kpe_r.astype(BF16)
    ckv_b = ckv_n.astype(BF16)
    k_nope = _dot(ckv_b, wuk_ref[...])
    vt_ref[...] = _dot_nt(wuvt_ref[...], ckv_b).astype(BF16)
    for h in range(MLA_HEADS):
        o = h * MLA_QKW
        q_ref[:, o:o + MLA_NOPE] = (qf[:, o:o + MLA_NOPE] * scale).astype(BF16)
        q_ref[:, o + MLA_NOPE:o + MLA_QKW] = (_rope128(qf[:, o + MLA_NOPE:o + MLA_QKW], cos_t, sin_t) * scale).astype(BF16)
        k_ref[:, o:o + MLA_NOPE] = k_nope[:, h * MLA_NOPE:(h + 1) * MLA_NOPE].astype(BF16)
        k_ref[:, o + MLA_NOPE:o + MLA_QKW] = kpe_b


def _mla_pre(proj, cos_t, sin_t, g_q, g_kv, w_uq, w_uk, w_uv_t):
    rows = proj.shape[0]
    tm = _tile(rows, 256, LANE)
    qkw = MLA_HEADS * MLA_QKW
    vw = MLA_HEADS * MLA_V
    row = lambda i: (i, 0)
    const = lambda i: (0, 0)
    return pl.pallas_call(
        _mla_pre_body,
        grid=(rows // tm,),
        in_specs=[
            pl.BlockSpec((tm, MLA_Q_LORA), lambda i: (i, AB_CQ // MLA_Q_LORA)),
            pl.BlockSpec((tm, MLA_KV_LORA), lambda i: (i, AB_CKV // MLA_KV_LORA)),
            pl.BlockSpec((tm, LANE), lambda i: (i, AB_KPE // LANE)),
            pl.BlockSpec((tm, LANE), row),
            pl.BlockSpec((tm, LANE), row),
            pl.BlockSpec((1, MLA_Q_LORA), const),
            pl.BlockSpec((1, MLA_KV_LORA), const),
            pl.BlockSpec((MLA_Q_LORA, qkw), const),
            pl.BlockSpec((MLA_KV_LORA, MLA_HEADS * MLA_NOPE), const),
            pl.BlockSpec((vw, MLA_KV_LORA), const),
        ],
        out_specs=[
            pl.BlockSpec((tm, qkw), row),
            pl.BlockSpec((tm, qkw), row),
            pl.BlockSpec((vw, tm), lambda i: (0, i)),
            pl.BlockSpec((tm, MLA_KV_LORA), row),
            pl.BlockSpec((tm, LANE), row),
        ],
        out_shape=[
            jax.ShapeDtypeStruct((rows, qkw), BF16),
            jax.ShapeDtypeStruct((rows, qkw), BF16),
            jax.ShapeDtypeStruct((vw, rows), BF16),
            jax.ShapeDtypeStruct((rows, MLA_KV_LORA), F32),
            jax.ShapeDtypeStruct((rows, LANE), F32),
        ],
        compiler_params=_params("parallel"),
        name="mla_pre",
    )(proj, proj, proj, cos_t, sin_t, g_q, g_kv, w_uq, w_uk, w_uv_t)


def _flash_body(q_ref, k_ref, vt_ref, dst_ref, o_ref, m_ref, l_ref, acc_ref, *, blk):
    del dst_ref
    qi = pl.program_id(1)
    n_heads = q_ref.shape[1] // MLA_QKW
    m_ref[...] = jnp.full(m_ref.shape, -jnp.inf, F32)
    l_ref[...] = jnp.zeros(l_ref.shape, F32)
    acc_ref[...] = jnp.zeros(acc_ref.shape, F32)

    def block(ki, masked):
        start = pl.multiple_of(ki * blk, blk)
        for h in range(n_heads):
            qk = slice(h * MLA_QKW, (h + 1) * MLA_QKW)
            st = _dot_nt(k_ref[pl.ds(start, blk), qk], q_ref[:, qk])
            if masked:
                k_chunk = lax.broadcasted_iota(jnp.int32, (blk, blk), 0) // CHUNK
                q_chunk = lax.broadcasted_iota(jnp.int32, (blk, blk), 1) // CHUNK
                st = jnp.where(k_chunk <= q_chunk, st, -jnp.inf)
            m_old = m_ref[h]
            m_new = jnp.maximum(m_old, jnp.max(st, axis=0, keepdims=True))
            alpha = jnp.exp2(m_old - m_new)
            p = jnp.exp2(st - m_new)
            l_ref[h] = alpha * l_ref[h] + jnp.sum(p, axis=0, keepdims=True)
            vt = vt_ref[h * MLA_V:(h + 1) * MLA_V, pl.ds(start, blk)]
            acc_ref[h] = alpha * acc_ref[h] + _dot(vt, p.astype(BF16))
            m_ref[h] = m_new

    def interior(ki, carry):
        block(ki, False)
        return carry

    lax.fori_loop(0, qi, interior, 0)
    block(qi, True)
    for h in range(n_heads):
        o_ref[:, h * MLA_V:(h + 1) * MLA_V] = (acc_ref[h] / l_ref[h]).T.astype(o_ref.dtype)


def _flash(q, k, vt, seq_len, tile, dst):
    blk = _tile(seq_len, tile, LANE)
    assert blk % CHUNK == 0
    hp = MLA_HEADS_PER_STEP
    return pl.pallas_call(
        functools.partial(_flash_body, blk=blk),
        grid=(MLA_HEADS // hp, seq_len // blk),
        in_specs=[
            pl.BlockSpec((blk, hp * MLA_QKW), lambda h, qi: (qi, h)),
            pl.BlockSpec((seq_len, hp * MLA_QKW), lambda h, qi: (0, h), pipeline_mode=pl.Buffered(1)),
            pl.BlockSpec((hp * MLA_V, seq_len), lambda h, qi: (h, 0), pipeline_mode=pl.Buffered(1)),
            pl.BlockSpec(memory_space=pl.ANY),
        ],
        out_specs=pl.BlockSpec((blk, hp * MLA_V), lambda h, qi: (qi, h)),
        out_shape=jax.ShapeDtypeStruct(dst.shape, dst.dtype),
        scratch_shapes=[pltpu.VMEM((hp, 1, blk), F32), pltpu.VMEM((hp, 1, blk), F32),
                        pltpu.VMEM((hp, MLA_V, blk), F32)],
        input_output_aliases={3: 0},
        compiler_params=_params("parallel", "arbitrary"),
        name="mla_flash",
    )(q, k, vt, dst)


def _mla_hist_body(q_ref, kn_ref, ckvn_ref, ckv_ref, kpe_ref, wuk_ref, wuv_ref, dst_ref, o_ref, kpe_pad_ref, *, past_len):
    del dst_ref
    t_new = q_ref.shape[0]
    v_new = _dot(ckvn_ref[...].astype(BF16), wuv_ref[...]).astype(BF16)
    ckv_b = ckv_ref[0].astype(BF16)
    k_past = _dot(ckv_b, wuk_ref[...]).astype(BF16)
    v_past = _dot(ckv_b, wuv_ref[...]).astype(BF16)
    kpe_pad_ref[:, :MLA_ROPE] = kpe_ref[0].astype(BF16)
    kpe_pad_ref[:, MLA_ROPE:] = jnp.zeros((past_len, LANE - MLA_ROPE), BF16)
    kpe_past = kpe_pad_ref[...]

    q_chunk = (past_len + lax.broadcasted_iota(jnp.int32, (t_new, past_len), 0)) // CHUNK
    mask_past = lax.broadcasted_iota(jnp.int32, (t_new, past_len), 1) // CHUNK <= q_chunk
    q_chunk_n = (past_len + lax.broadcasted_iota(jnp.int32, (t_new, t_new), 0)) // CHUNK
    mask_new = (past_len + lax.broadcasted_iota(jnp.int32, (t_new, t_new), 1)) // CHUNK <= q_chunk_n

    for h in range(MLA_HEADS):
        o = h * MLA_QKW
        q_nope = q_ref[:, o:o + MLA_NOPE]
        q_pe = q_ref[:, o + MLA_NOPE:o + MLA_QKW]
        s_past = _dot_nt(q_nope, k_past[:, h * MLA_NOPE:(h + 1) * MLA_NOPE]) + _dot_nt(q_pe, kpe_past)
        s_new = _dot_nt(q_ref[:, o:o + MLA_QKW], kn_ref[:, o:o + MLA_QKW])
        s_past = jnp.where(mask_past, s_past, -jnp.inf)
        s_new = jnp.where(mask_new, s_new, -jnp.inf)
        m = jnp.maximum(jnp.max(s_past, axis=-1, keepdims=True), jnp.max(s_new, axis=-1, keepdims=True))
        p_past = jnp.exp2(s_past - m)
        p_new = jnp.exp2(s_new - m)
        denom = jnp.sum(p_past, axis=-1, keepdims=True) + jnp.sum(p_new, axis=-1, keepdims=True)
        pv = _dot(p_past.astype(BF16), v_past[:, h * MLA_V:(h + 1) * MLA_V])
        pv = pv + _dot(p_new.astype(BF16), v_new[:, h * MLA_V:(h + 1) * MLA_V])
        o_ref[:, h * MLA_V:(h + 1) * MLA_V] = (pv / denom).astype(o_ref.dtype)


def _mla_hist(q, k, ckv_n, row0, n_seq, t_new, ckv_past, kpe_past, w_uk, w_uv, dst):
    past_len = ckv_past.shape[1]
    base = row0 // t_new
    qkw = MLA_HEADS * MLA_QKW
    vw = MLA_HEADS * MLA_V
    row = lambda b: (base + b, 0)
    const = lambda b: (0, 0)
    return pl.pallas_call(
        functools.partial(_mla_hist_body, past_len=past_len),
        grid=(n_seq,),
        in_specs=[
            pl.BlockSpec((t_new, qkw), row),
            pl.BlockSpec((t_new, qkw), row),
            pl.BlockSpec((t_new, MLA_KV_LORA), row),
            pl.BlockSpec((1, past_len, MLA_KV_LORA), lambda b: (b, 0, 0)),
            pl.BlockSpec((1, past_len, MLA_ROPE), lambda b: (b, 0, 0)),
            pl.BlockSpec((MLA_KV_LORA, MLA_HEADS * MLA_NOPE), const),
            pl.BlockSpec((MLA_KV_LORA, vw), const),
            pl.BlockSpec(memory_space=pl.ANY),
        ],
        out_specs=pl.BlockSpec((t_new, vw), row),
        out_shape=jax.ShapeDtypeStruct(dst.shape, dst.dtype),
        scratch_shapes=[pltpu.VMEM((past_len, LANE), BF16)],
        input_output_aliases={7: 0},
        compiler_params=_params("parallel"),
        name="mla_hist",
    )(q, k, ckv_n, ckv_past, kpe_past, w_uk, w_uv, dst)


SSD_L = 128
SSD_GROUPS_PER_STEP = 4


def _conv_silu(x_ref, buf_ref, w_ref, b_ref, ext_ref, first):
    L = x_ref.shape[0]

    @pl.when(first)
    def _():
        ext_ref[8 - (SSD_CONV - 1):8, :] = buf_ref[0]

    @pl.when(jnp.logical_not(first))
    def _():
        ext_ref[0:8, :] = ext_ref[L:L + 8, :]

    ext_ref[8:8 + L, :] = x_ref[...]
    y = b_ref[...]
    for j in range(SSD_CONV):
        y = y + ext_ref[8 - (SSD_CONV - 1) + j:8 - (SSD_CONV - 1) + j + L, :] * w_ref[j:j + 1, :]
    return _silu(y)


def _pad_rows(x, n):
    if x.shape[0] == n:
        return x
    return jnp.concatenate([x, jnp.zeros((n - x.shape[0], x.shape[1]), x.dtype)], axis=0)


def _ssd_body(z_ref, x_ref, bm_ref, cm_ref, dt_ref, bufx_ref, bufb_ref, bufc_ref, wx_ref, wb_ref, wc_ref,
              bx_ref, bb_ref, bc_ref, dtb_ref, alog_ref, dsk_ref, gn_ref, s0_ref, dst_ref,
              y_ref, s_out_ref, s_ref, extx_ref, extb_ref, extc_ref):
    del dst_ref
    c = pl.program_id(2)
    first = c == 0
    L = x_ref.shape[0]
    LS = SSD_L
    n_grp = x_ref.shape[1] // SSD_GW

    @pl.when(first)
    def _():
        for p in range(n_grp):
            s_ref[p] = s0_ref[0, p * SSD_REP:(p + 1) * SSD_REP].reshape(SSD_GW, SSD_N).T

    xs_all = _conv_silu(x_ref, bufx_ref, wx_ref, bx_ref, extx_ref, first)
    bm_all = _conv_silu(bm_ref, bufb_ref, wb_ref, bb_ref, extb_ref, first)
    cm_all = _conv_silu(cm_ref, bufc_ref, wc_ref, bc_ref, extc_ref, first)

    row = lax.broadcasted_iota(jnp.int32, (LS, LS), 0)
    col = lax.broadcasted_iota(jnp.int32, (LS, LS), 1)
    tri = jnp.where(row >= col, 1.0, 0.0).astype(BF16)
    erow = lax.broadcasted_iota(jnp.int32, (LANE, SSD_GW), 0)
    ecol = lax.broadcasted_iota(jnp.int32, (LANE, SSD_GW), 1) // SSD_HEADDIM
    expand = jnp.where(erow == ecol, 1.0, 0.0).astype(BF16)
    causal = lax.broadcasted_iota(jnp.int32, (L, LS), 0) >= lax.broadcasted_iota(jnp.int32, (L, LS), 1)
    lane_half = lax.broadcasted_iota(jnp.int32, (LS, LANE), 1) < SSD_HEADDIM

    new_states = []
    for p in range(n_grp):
        gw = slice(p * SSD_GW, (p + 1) * SSD_GW)
        gn = slice(p * SSD_N, (p + 1) * SSD_N)
        xs = xs_all[:, gw]
        cm_b = cm_all[:, gn].astype(BF16)
        dt = _softplus(dt_ref[:, p * LANE:(p + 1) * LANE] + dtb_ref[p])
        xs_p = _pad_rows(xs, LS)
        bm_p = _pad_rows(bm_all[:, gn], LS).astype(BF16)
        dt_p = _pad_rows(dt, LS)
        da = dt_p * (-jnp.exp(alog_ref[p]))

        cum = _dot_exact_rhs(tri, da)
        cum_t = cum.T
        dt_t = dt_p.T
        cum_q = cum[:L]
        last = cum[LS - 1:LS, :]
        e_cum = _dot_exact_lhs(jnp.exp(cum_q), expand)
        w_dec = _dot_exact_lhs(jnp.exp(last - cum) * dt_p, expand)
        e_last = _dot_exact_lhs(jnp.exp(last), expand)
        cb = _dot_nt(cm_b, bm_p)

        def head_weights(r):
            diff = cum_q[:, r:r + 1] - cum_t[r:r + 1, :]
            return (cb * jnp.exp(jnp.where(causal, diff, -jnp.inf)) * dt_t[r:r + 1, :]).astype(BF16)

        y_parts = []
        for q in range(SSD_REP // 2):
            w_pair = jnp.concatenate([head_weights(2 * q), head_weights(2 * q + 1)], axis=1)
            x_pair = xs_p[:, q * LANE:(q + 1) * LANE]
            x_bd = jnp.concatenate([jnp.where(lane_half, x_pair, 0.0), jnp.where(lane_half, 0.0, x_pair)], axis=0)
            y_parts.append(_dot(w_pair, x_bd.astype(BF16)))
        y = jnp.concatenate(y_parts, axis=1)

        s_old = s_ref[p]
        y = y + _dot(cm_b, s_old.astype(BF16)) * e_cum
        s_new = s_old * e_last + _dot_tn(bm_p, (xs_p * w_dec).astype(BF16))
        s_ref[p] = s_new
        new_states.append(s_new)

        y = y + xs * dsk_ref[:, gw]
        y = y * _silu(z_ref[:, gw])
        y_ref[:, gw] = _rms(y, gn_ref[:, gw]).astype(y_ref.dtype)

    @pl.when(c == pl.num_programs(2) - 1)
    def _():
        for p in range(n_grp):
            s_out_ref[0, p * SSD_REP:(p + 1) * SSD_REP] = new_states[p].T.reshape(SSD_REP, SSD_HEADDIM, SSD_N)


def _ssd(proj, row0, n_seq, seq_len, chunk, conv_buf, s0, w_conv, b_conv, dt_bias, a_log, d_skip, g_norm, dst):
    d_inner = SSD_GROUPS * SSD_GW
    L = chunk
    n_chunks = seq_len // L
    base = row0 // L
    bn = SSD_GROUPS * SSD_N
    gp = SSD_GROUPS_PER_STEP
    gw, gn, gl = gp * SSD_GW, gp * SSD_N, gp * LANE
    off_x, off_b, off_c, off_dt = d_inner, 2 * d_inner, 2 * d_inner + bn, 2 * d_inner + 2 * bn

    def rmap(col_blocks_off):
        return lambda b, g, c: (base + b * n_chunks + c, col_blocks_off + g)

    def cmap(col_blocks_off):
        return lambda b, g, c: (0, col_blocks_off + g)

    nb = SSD_CONV - 1
    in_specs = [
        pl.BlockSpec((L, gw), rmap(0)),
        pl.BlockSpec((L, gw), rmap(off_x // gw)),
        pl.BlockSpec((L, gn), rmap(off_b // gn)),
        pl.BlockSpec((L, gn), rmap(off_c // gn)),
        pl.BlockSpec((L, gl), rmap(off_dt // gl)),
        pl.BlockSpec((1, nb, gw), lambda b, g, c: (b, 0, g)),
        pl.BlockSpec((1, nb, gn), lambda b, g, c: (b, 0, d_inner // gn + g)),
        pl.BlockSpec((1, nb, gn), lambda b, g, c: (b, 0, (d_inner + bn) // gn + g)),
        pl.BlockSpec((SSD_CONV, gw), cmap(0)),
        pl.BlockSpec((SSD_CONV, gn), cmap(d_inner // gn)),
        pl.BlockSpec((SSD_CONV, gn), cmap((d_inner + bn) // gn)),
        pl.BlockSpec((1, gw), cmap(0)),
        pl.BlockSpec((1, gn), cmap(d_inner // gn)),
        pl.BlockSpec((1, gn), cmap((d_inner + bn) // gn)),
        pl.BlockSpec((gp, 1, LANE), lambda b, g, c: (g, 0, 0)),
        pl.BlockSpec((gp, 1, LANE), lambda b, g, c: (g, 0, 0)),
        pl.BlockSpec((1, gw), cmap(0)),
        pl.BlockSpec((1, gw), cmap(0)),
        pl.BlockSpec((1, gp * SSD_REP, SSD_HEADDIM, SSD_N), lambda b, g, c: (b, g, 0, 0)),
        pl.BlockSpec(memory_space=pl.ANY),
    ]
    args = [proj] * 5 + [conv_buf] * 3 + [w_conv] * 3 + [b_conv] * 3 + [dt_bias, a_log, d_skip, g_norm, s0, dst]
    return pl.pallas_call(
        _ssd_body,
        grid=(n_seq, SSD_GROUPS // gp, n_chunks),
        in_specs=in_specs,
        out_specs=[
            pl.BlockSpec((L, gw), rmap(0)),
            pl.BlockSpec((1, gp * SSD_REP, SSD_HEADDIM, SSD_N), lambda b, g, c: (b, g, 0, 0)),
        ],
        out_shape=[
            jax.ShapeDtypeStruct(dst.shape, dst.dtype),
            jax.ShapeDtypeStruct((n_seq, SSD_GROUPS * SSD_REP, SSD_HEADDIM, SSD_N), F32),
        ],
        scratch_shapes=[
            pltpu.VMEM((gp, SSD_N, SSD_GW), F32),
            pltpu.VMEM((L + 8, gw), F32),
            pltpu.VMEM((L + 8, gn), F32),
            pltpu.VMEM((L + 8, gn), F32),
        ],
        input_output_aliases={len(args) - 1: 0},
        compiler_params=_params("parallel", "parallel", "arbitrary"),
        name="ssd",
    )(*args)


def _rot_cols(w):
    half = w.shape[-1] // 2
    return jnp.concatenate([-w[..., half:], w[..., :half]], axis=-1)


def _prep_ab_in(w):
    d = w.shape[0]
    o = 0
    parts = {}
    for name, size in (("q", GLA_QK), ("k", GLA_QK), ("v", GLA_VW), ("gr", GLA_GATE_RANK), ("og", GLA_VW),
                       ("cq", MLA_Q_LORA), ("ckv", MLA_KV_LORA), ("kpe", MLA_ROPE)):
        parts[name] = w[:, o:o + size]
        o += size
    zeros = jnp.zeros((d, LANE - GLA_GATE_RANK), w.dtype)
    out = jnp.concatenate([parts["q"], parts["k"], parts["v"], parts["og"], parts["cq"], parts["ckv"],
                           parts["kpe"], _rot_cols(parts["kpe"]), parts["gr"], zeros], axis=1)
    assert out.shape[1] == AB_W
    return out.astype(BF16)


def _prep_uq(w):
    w = w.reshape(w.shape[0], MLA_HEADS, MLA_NOPE + MLA_ROPE)
    pe = w[..., MLA_NOPE:]
    out = jnp.concatenate([w[..., :MLA_NOPE], pe, _rot_cols(pe)], axis=-1)
    return out.reshape(w.shape[0], MLA_HEADS * MLA_QKW).astype(BF16)


def _prep_ssd_in(w, d_inner):
    d = w.shape[0]
    n_heads = SSD_GROUPS * SSD_REP
    main = w[:, :2 * d_inner + 2 * SSD_GROUPS * SSD_N]
    dt = w[:, 2 * d_inner + 2 * SSD_GROUPS * SSD_N:].reshape(d, SSD_GROUPS, SSD_REP)
    assert dt.shape[1] * dt.shape[2] == n_heads
    dt = jnp.concatenate([dt, jnp.zeros((d, SSD_GROUPS, LANE - SSD_REP), w.dtype)], axis=-1)
    return jnp.concatenate([main, dt.reshape(d, SSD_GROUPS * LANE)], axis=1).astype(BF16)


def _head_table(v):
    v = v.reshape(SSD_GROUPS, 1, SSD_REP)
    return jnp.concatenate([v, jnp.zeros((SSD_GROUPS, 1, LANE - SSD_REP), v.dtype)], axis=-1)


def _rope_tables(pos):
    half = MLA_ROPE // 2
    inv = ROPE_THETA ** (-jnp.arange(half, dtype=F32) / half)
    ang = pos.astype(F32)[:, None] * inv[None, :]
    zeros = jnp.zeros((pos.shape[0], LANE - MLA_ROPE), F32)
    cos = jnp.cos(ang)
    sin = jnp.sin(ang)
    return jnp.concatenate([cos, cos, zeros], axis=1), jnp.concatenate([sin, sin, zeros], axis=1)


def kernel(x_prompt, x_sample, cache_mla_ckv, cache_mla_kpe, state_gla, state_ssd, state_ssd_conv, g_ffn1, w_ffn1_gate, w_ffn1_up, w_ffn1_down, g_mix, g_ffn2, w_ffn2_gate, w_ffn2_up, w_ffn2_down, w_ab_in, w_gla_gate_up, b_gla_gate, g_gla_norm, g_mla_q_norm, w_mla_uq, g_mla_kv_norm, w_mla_uk, w_mla_uv, w_ab_out, w_ssd_in, w_ssd_conv, b_ssd_conv, ssd_dt_bias, ssd_a_log, ssd_d, g_ssd_norm, w_ssd_out, g_final):
    n_p, t_p, d = x_prompt.shape
    n_s, t_s, _ = x_sample.shape
    assert n_p == 1
    depth = g_ffn1.shape[0]
    past_len = cache_mla_ckv.shape[2]
    rows_p = n_p * t_p
    d_inner = w_ssd_out.shape[1]
    assert d_inner == SSD_GROUPS * SSD_GW

    x = jnp.concatenate([x_prompt.reshape(rows_p, d), x_sample.reshape(n_s * t_s, d)], axis=0)
    rows = rows_p + n_s * t_s
    pos = jnp.concatenate([jnp.arange(t_p, dtype=jnp.int32),
                           jnp.tile(past_len + jnp.arange(t_s, dtype=jnp.int32), n_s)])
    cos_t, sin_t = _rope_tables(pos)
    row2 = lambda v: v.reshape(1, -1)

    outs = {k: [] for k in ("p_ckv", "p_kpe", "p_gla", "p_ssd", "p_conv", "s_ckv", "s_kpe", "s_gla", "s_ssd", "s_conv")}
    for layer in range(depth):
        i = layer // 2
        x = _ffn(x, row2(g_ffn1[layer]), w_ffn1_gate, w_ffn1_up, w_ffn1_down, layer)
        if layer % 2 == 0:
            proj = _norm_proj(x, row2(g_mix[layer]), _prep_ab_in(w_ab_in[i]), AB_W // 2)
            w_gate = jnp.concatenate([w_gla_gate_up[i], jnp.zeros((LANE - GLA_GATE_RANK, GLA_QK), F32)], axis=0).astype(BF16)
            gla_args = (w_gate, row2(b_gla_gate[i]), row2(g_gla_norm[i]))
            gla_chunk = min(CHUNK, t_p)
            o_gla, p_gla = _gla(proj, 0, n_p, t_p, gla_chunk, min(4, t_p // gla_chunk),
                                jnp.zeros((n_p, GLA_HEADS, GLA_DK, GLA_DV), F32), *gla_args,
                                dst=jnp.zeros((rows, GLA_VW), BF16))
            o_gla, s_gla = _gla(proj, rows_p, n_s, t_s, t_s, 1, state_gla[i], *gla_args, dst=o_gla)

            w_uk = w_mla_uk[i].astype(BF16)
            w_uv = w_mla_uv[i].astype(BF16)
            q, k, vt, ckv_n, kpe_r = _mla_pre(proj, cos_t, sin_t, row2(g_mla_q_norm[i]), row2(g_mla_kv_norm[i]),
                                              _prep_uq(w_mla_uq[i]), w_uk, w_uv.T)
            o_mla = _flash(q, k, vt, t_p, 512, jnp.zeros((rows, MLA_HEADS * MLA_V), BF16))
            o_mla = _mla_hist(q, k, ckv_n, rows_p, n_s, t_s, cache_mla_ckv[i], cache_mla_kpe[i], w_uk, w_uv, o_mla)

            w_out = w_ab_out[i].astype(BF16)
            x = _out_proj(x, [o_gla, o_mla], [w_out[:GLA_VW], w_out[GLA_VW:]])
            outs["p_ckv"].append(ckv_n[:rows_p].reshape(n_p, t_p, MLA_KV_LORA))
            outs["s_ckv"].append(ckv_n[rows_p:].reshape(n_s, t_s, MLA_KV_LORA))
            outs["p_kpe"].append(kpe_r[:rows_p, :MLA_ROPE].reshape(n_p, t_p, MLA_ROPE))
            outs["s_kpe"].append(kpe_r[rows_p:, :MLA_ROPE].reshape(n_s, t_s, MLA_ROPE))
            outs["p_gla"].append(p_gla)
            outs["s_gla"].append(s_gla)
        else:
            n_heads = SSD_GROUPS * SSD_REP
            conv_ch = d_inner + 2 * SSD_GROUPS * SSD_N
            proj = _norm_proj(x, row2(g_mix[layer]), _prep_ssd_in(w_ssd_in[i], d_inner),
                              (2 * d_inner + 2 * SSD_GROUPS * SSD_N + SSD_GROUPS * LANE) // 8)
            ssd_args = (w_ssd_conv[i], row2(b_ssd_conv[i]), _head_table(ssd_dt_bias[i]), _head_table(ssd_a_log[i]),
                        row2(jnp.repeat(ssd_d[i], SSD_HEADDIM)), row2(g_ssd_norm[i]))
            y, p_ssd = _ssd(proj, 0, n_p, t_p, min(SSD_L, t_p), jnp.zeros((n_p, SSD_CONV - 1, conv_ch), F32),
                            jnp.zeros((n_p, n_heads, SSD_HEADDIM, SSD_N), F32), *ssd_args,
                            dst=jnp.zeros((rows, d_inner), BF16))
            y, s_ssd = _ssd(proj, rows_p, n_s, t_s, t_s, state_ssd_conv[i], state_ssd[i], *ssd_args, dst=y)
            x = _out_proj(x, [y], [w_ssd_out[i].astype(BF16)])
            nb = SSD_CONV - 1
            outs["p_conv"].append(jnp.stack([
                lax.slice(proj, (b * t_p + t_p - nb, d_inner), (b * t_p + t_p, d_inner + conv_ch)) for b in range(n_p)]))
            outs["s_conv"].append(jnp.stack([
                lax.slice(proj, (rows_p + b * t_s + t_s - nb, d_inner), (rows_p + b * t_s + t_s, d_inner + conv_ch))
                for b in range(n_s)]))
            outs["p_ssd"].append(p_ssd)
            outs["s_ssd"].append(s_ssd)
        x = _ffn(x, row2(g_ffn2[layer]), w_ffn2_gate, w_ffn2_up, w_ffn2_down, layer,
                 g_final=row2(g_final) if layer == depth - 1 else None)

    y_prompt = x[:rows_p].reshape(n_p, t_p, d)
    y_sample = x[rows_p:].reshape(n_s, t_s, d)
    st = {k: jnp.stack(v) for k, v in outs.items()}
    return (y_prompt, y_sample, st["p_ckv"], st["p_kpe"], st["p_gla"], st["p_ssd"], st["p_conv"],
            st["s_ckv"], st["s_kpe"], st["s_gla"], st["s_ssd"], st["s_conv"])
```

```python
import functools

import jax
import jax.numpy as jnp
from jax import lax
from jax.experimental import pallas as pl
from jax.experimental.pallas import tpu as pltpu

F32 = jnp.float32
BF16 = jnp.bfloat16

RMS_EPS = 1e-6
CHUNK = 64

GLA_HEADS = 4
GLA_DK = 128
GLA_DV = 256
GLA_GATE_RANK = 16
GLA_TAU = 16.0
GLA_SAFE_RANGE = 40.0
GLA_QK = GLA_HEADS * GLA_DK
GLA_VW = GLA_HEADS * GLA_DV

MLA_HEADS = 8
MLA_Q_LORA = 512
MLA_KV_LORA = 512
MLA_NOPE = 128
MLA_ROPE = 64
MLA_V = 128
ROPE_THETA = 10000.0
MLA_QKW = 256
MLA_SCORE_SCALE = (MLA_NOPE + MLA_ROPE) ** -0.5 * 1.4426950408889634
MLA_HEADS_PER_STEP = 4

SSD_HEADDIM = 64
SSD_GROUPS = 8
SSD_N = 128
SSD_CONV = 4
SSD_GW = 512
SSD_REP = SSD_GW // SSD_HEADDIM

LANE = 128
SUBLANE = 8
AB_Q, AB_K, AB_V, AB_OG, AB_CQ, AB_CKV, AB_KPE, AB_GR, AB_W = 0, 512, 1024, 2048, 3072, 3584, 4096, 4224, 4352

VMEM_LIMIT = 56 * 1024 * 1024


def _tile(n, target, align):
    best = None
    for t in range(align, min(n, target) + 1, align):
        if n % t == 0:
            best = t
    return best if best is not None else n


def _rms(x, g):
    return x * lax.rsqrt(jnp.mean(x * x, axis=-1, keepdims=True) + RMS_EPS) * g


def _silu(x):
    return x * jax.nn.sigmoid(x)


def _softplus(x):
    return jnp.maximum(x, 0.0) + jnp.log1p(jnp.exp(-jnp.abs(x)))


def _dot(a, b):
    return jnp.dot(a, b, preferred_element_type=F32)


def _dot_nt(a, b):
    return lax.dot_general(a, b, (((1,), (1,)), ((), ())), preferred_element_type=F32)


def _dot_tn(a, b):
    return lax.dot_general(a, b, (((0,), (0,)), ((), ())), preferred_element_type=F32)


def _split3(x):
    h = x.astype(BF16)
    r = x - h.astype(F32)
    m = r.astype(BF16)
    l = (r - m.astype(F32)).astype(BF16)
    return h, m, l


def _dot_exact_rhs(a_bf16, x):
    h, m, l = _split3(x)
    return _dot(a_bf16, h) + _dot(a_bf16, m) + _dot(a_bf16, l)


def _dot_exact_lhs(x, b_bf16):
    h, m, l = _split3(x)
    return _dot(h, b_bf16) + _dot(m, b_bf16) + _dot(l, b_bf16)


def _params(*sem):
    return pltpu.CompilerParams(dimension_semantics=sem, vmem_limit_bytes=VMEM_LIMIT)


def _ffn_body(x_ref, g_ref, wg_ref, wu_ref, wd_ref, gf_ref, o_ref, h_ref, *, final_norm):
    j = pl.program_id(1)

    @pl.when(j == 0)
    def _():
        x = x_ref[...]
        h_ref[...] = _rms(x, g_ref[...]).astype(BF16)
        o_ref[...] = x

    h = h_ref[...]
    a = _dot(h, wg_ref[...].astype(BF16))
    b = _dot(h, wu_ref[...].astype(BF16))
    act = (_silu(a) * b) * 0.5
    o_ref[...] += _dot(act.astype(BF16), wd_ref[...].astype(BF16))

    if final_norm:
        @pl.when(j == pl.num_programs(1) - 1)
        def _():
            o_ref[...] = _rms(o_ref[...], gf_ref[...])


def _ffn(x, g, wg, wu, wd, layer, g_final=None):
    rows, d = x.shape
    f = wg.shape[2]
    tm = _tile(rows, 1056, 8)
    tf = _tile(f, 256, LANE)
    final = g_final is not None
    return pl.pallas_call(
        functools.partial(_ffn_body, final_norm=final),
        grid=(rows // tm, f // tf),
        in_specs=[
            pl.BlockSpec((tm, d), lambda i, j: (i, 0), pipeline_mode=pl.Buffered(1)),
            pl.BlockSpec((1, d), lambda i, j: (0, 0)),
            pl.BlockSpec((None, d, tf), lambda i, j: (layer, 0, j)),
            pl.BlockSpec((None, d, tf), lambda i, j: (layer, 0, j)),
            pl.BlockSpec((None, tf, d), lambda i, j: (layer, j, 0)),
            pl.BlockSpec((1, d), lambda i, j: (0, 0)),
        ],
        out_specs=pl.BlockSpec((tm, d), lambda i, j: (i, 0)),
        out_shape=jax.ShapeDtypeStruct((rows, d), F32),
        scratch_shapes=[pltpu.VMEM((tm, d), BF16)],
        compiler_params=_params("parallel", "arbitrary"),
        name="ffn",
    )(x, g, wg, wu, wd, g_final if final else g)


def _norm_proj_body(x_ref, g_ref, w_ref, o_ref, h_ref):
    @pl.when(pl.program_id(1) == 0)
    def _():
        h_ref[...] = _rms(x_ref[...], g_ref[...]).astype(BF16)

    o_ref[...] = _dot(h_ref[...], w_ref[...])


def _norm_proj(x, g, w, tn):
    rows, d = x.shape
    n = w.shape[1]
    tm = _tile(rows, 768, 8)
    return pl.pallas_call(
        _norm_proj_body,
        grid=(rows // tm, n // tn),
        in_specs=[
            pl.BlockSpec((tm, d), lambda i, j: (i, 0)),
            pl.BlockSpec((1, d), lambda i, j: (0, 0)),
            pl.BlockSpec((d, tn), lambda i, j: (0, j)),
        ],
        out_specs=pl.BlockSpec((tm, tn), lambda i, j: (i, j)),
        out_shape=jax.ShapeDtypeStruct((rows, n), F32),
        scratch_shapes=[pltpu.VMEM((tm, d), BF16)],
        compiler_params=_params("parallel", "arbitrary"),
        name="norm_proj",
    )(x, g, w)


def _out_proj_body(*refs, n_in):
    x_ref = refs[0]
    o_ref = refs[1 + 2 * n_in]
    acc = x_ref[...]
    for i in range(n_in):
        acc = acc + _dot(refs[1 + i][...], refs[1 + n_in + i][...])
    o_ref[...] = acc


def _out_proj(x, ys, ws):
    rows, d = x.shape
    n_in = len(ys)
    tm = _tile(rows, 768, 8)
    tn = _tile(d, 1024, LANE)
    in_specs = [pl.BlockSpec((tm, tn), lambda i, j: (i, j))]
    in_specs += [pl.BlockSpec((tm, y.shape[1]), lambda i, j: (i, 0)) for y in ys]
    in_specs += [pl.BlockSpec((w.shape[0], tn), lambda i, j: (0, j)) for w in ws]
    return pl.pallas_call(
        functools.partial(_out_proj_body, n_in=n_in),
        grid=(rows // tm, d // tn),
        in_specs=in_specs,
        out_specs=pl.BlockSpec((tm, tn), lambda i, j: (i, j)),
        out_shape=jax.ShapeDtypeStruct((rows, d), F32),
        compiler_params=_params("parallel", "arbitrary"),
        name="out_proj",
    )(x, *ys, *ws)


def _gla_body(q_ref, k_ref, v_ref, og_ref, gr_ref, s0_ref, wg_ref, bg_ref, gn_ref, *rest, aliased, L):
    o_ref, s_out_ref, s_ref, b_scr = rest[1:] if aliased else rest
    step = pl.program_id(1)
    n_sub = q_ref.shape[0] // L

    @pl.when(step == 0)
    def _():
        for h in range(GLA_HEADS):
            s_ref[h] = s0_ref[0, h].T

    pre = _dot(gr_ref[...].astype(BF16), wg_ref[...]) + bg_ref[...]
    log_a = (jnp.minimum(pre, 0.0) - jnp.log1p(jnp.exp(-jnp.abs(pre)))) * (1.0 / GLA_TAU)
    causal = lax.broadcasted_iota(jnp.int32, (L, L), 0) >= lax.broadcasted_iota(jnp.int32, (L, L), 1)
    tri = jnp.where(causal, 1.0, 0.0).astype(BF16)
    gn = gn_ref[...]

    cums = [_dot_exact_rhs(tri, log_a[c * L:(c + 1) * L, :]) for c in range(n_sub)]
    total = cums[0][L - 1:L, :]
    for b_c in cums[1:]:
        total = jnp.minimum(total, b_c[L - 1:L, :])
    mild = jnp.min(total) >= -GLA_SAFE_RANGE

    def intra_factored(c, h, q, b):
        r0 = c * L
        b_mid = b[L // 2 - 1:L // 2, :]
        q_mid = (q * jnp.exp(b - b_mid)).astype(BF16)
        k_mid = (k_ref[r0:r0 + L, h * GLA_DK:(h + 1) * GLA_DK] * jnp.exp(b_mid - b)).astype(BF16)
        att = jnp.where(causal, _dot_nt(q_mid, k_mid), 0.0)
        return _dot(att.astype(BF16), v_ref[r0:r0 + L, h * GLA_DV:(h + 1) * GLA_DV].astype(BF16))

    def intra_pairwise(c, h, q, b):
        r0 = c * L
        b_scr[...] = b
        t_idx = lax.broadcasted_iota(jnp.int32, (L, 1), 0)

        def add_keys(g, o):
            s0 = pl.multiple_of(g * SUBLANE, SUBLANE)
            k_g = k_ref[pl.ds(r0 + s0, SUBLANE), h * GLA_DK:(h + 1) * GLA_DK]
            v_g = v_ref[pl.ds(r0 + s0, SUBLANE), h * GLA_DV:(h + 1) * GLA_DV]
            b_g = b_scr[pl.ds(s0, SUBLANE), :]
            for j in range(SUBLANE):
                decay = jnp.exp(jnp.minimum(b - b_g[j:j + 1, :], 0.0))
                w = jnp.sum(q * k_g[j:j + 1, :] * decay, axis=-1, keepdims=True)
                o = o + jnp.where(t_idx >= s0 + j, w, 0.0) * v_g[j:j + 1, :]
            return o

        return lax.fori_loop(0, L // SUBLANE, add_keys, jnp.zeros((L, GLA_DV), F32))

    def run(intra):
        states = [s_ref[h] for h in range(GLA_HEADS)]
        for c in range(n_sub):
            r0 = c * L
            for h in range(GLA_HEADS):
                kc = slice(h * GLA_DK, (h + 1) * GLA_DK)
                vc = slice(h * GLA_DV, (h + 1) * GLA_DV)
                b = cums[c][:, kc]
                b_last = b[L - 1:L, :]
                q = q_ref[r0:r0 + L, kc] * (GLA_DK ** -0.5)
                q_in = (q * jnp.exp(b)).astype(BF16)
                k_end = (k_ref[r0:r0 + L, kc] * jnp.exp(b_last - b)).astype(BF16)
                s_old = states[h]
                o = intra(c, h, q, b) + _dot_nt(q_in, s_old.astype(BF16))
                states[h] = s_old * jnp.exp(b_last) + _dot_tn(v_ref[r0:r0 + L, vc].astype(BF16), k_end)
                o_ref[r0:r0 + L, vc] = (_rms(o, gn) * _silu(og_ref[r0:r0 + L, vc])).astype(o_ref.dtype)

        for h in range(GLA_HEADS):
            s_ref[h] = states[h]

        @pl.when(step == pl.num_programs(1) - 1)
        def _():
            for h in range(GLA_HEADS):
                s_out_ref[0, h] = states[h].T

    @pl.when(mild)
    def _():
        run(intra_factored)

    @pl.when(jnp.logical_not(mild))
    def _():
        run(intra_pairwise)


def _gla(proj, row0, n_seq, seq_len, chunk, n_sub, s0, w_gate, b_gate, g_norm, dst=None):
    rows = proj.shape[0]
    tr = chunk * n_sub
    n_steps = seq_len // tr
    base = row0 // tr

    def rmap(col_block):
        return lambda b, s: (base + b * n_steps + s, col_block)

    const = lambda b, s: (0, 0)
    in_specs = [
        pl.BlockSpec((tr, GLA_QK), rmap(AB_Q // GLA_QK)),
        pl.BlockSpec((tr, GLA_QK), rmap(AB_K // GLA_QK)),
        pl.BlockSpec((tr, GLA_VW), rmap(AB_V // GLA_VW)),
        pl.BlockSpec((tr, GLA_VW), rmap(AB_OG // GLA_VW)),
        pl.BlockSpec((tr, LANE), rmap(AB_GR // LANE)),
        pl.BlockSpec((1, GLA_HEADS, GLA_DK, GLA_DV), lambda b, s: (b, 0, 0, 0)),
        pl.BlockSpec((LANE, GLA_QK), const),
        pl.BlockSpec((1, GLA_QK), const),
        pl.BlockSpec((1, GLA_DV), const),
    ]
    args = [proj, proj, proj, proj, proj, s0, w_gate, b_gate, g_norm]
    aliases = {}
    if dst is not None:
        in_specs.append(pl.BlockSpec(memory_space=pl.ANY))
        args.append(dst)
        aliases = {len(args) - 1: 0}
    return pl.pallas_call(
        functools.partial(_gla_body, aliased=dst is not None, L=chunk),
        grid=(n_seq, n_steps),
        in_specs=in_specs,
        out_specs=[
            pl.BlockSpec((tr, GLA_VW), rmap(0)),
            pl.BlockSpec((1, GLA_HEADS, GLA_DK, GLA_DV), lambda b, s: (b, 0, 0, 0)),
        ],
        out_shape=[
            jax.ShapeDtypeStruct((rows, GLA_VW), BF16),
            jax.ShapeDtypeStruct((n_seq, GLA_HEADS, GLA_DK, GLA_DV), F32),
        ],
        scratch_shapes=[pltpu.VMEM((GLA_HEADS, GLA_DV, GLA_DK), F32), pltpu.VMEM((chunk, GLA_DK), F32)],
        input_output_aliases=aliases,
        compiler_params=_params("parallel", "arbitrary"),
        name="gla",
    )(*args)


def _rope128(x, cos_t, sin_t):
    return x * cos_t + pltpu.roll(x, MLA_ROPE, 1) * sin_t


def _mla_pre_body(cq_ref, ckv_ref, kpe_ref, cos_ref, sin_ref, gq_ref, gkv_ref, wuq_ref, wuk_ref, wuvt_ref,
                  q_ref, k_ref, vt_ref, ckvn_ref, kper_ref):
    scale = MLA_SCORE_SCALE
    cos_t = cos_ref[...]
    sin_t = sin_ref[...]
    qf = _dot(_rms(cq_ref[...], gq_ref[...]).astype(BF16), wuq_ref[...])
    ckv_n = _rms(ckv_ref[...], gkv_ref[...])
    ckvn_ref[...] = ckv_n
    kpe_r = _rope128(kpe_ref[...], cos_t, sin_t)
    kper_ref[...] = kpe_r
    kpe_b = kpe_r.astype(BF16)
    ckv_b = ckv_n.astype(BF16)
    k_nope = _dot(ckv_b, wuk_ref[...])
    vt_ref[...] = _dot_nt(wuvt_ref[...], ckv_b).astype(BF16)
    for h in range(MLA_HEADS):
        o = h * MLA_QKW
        q_ref[:, o:o + MLA_NOPE] = (qf[:, o:o + MLA_NOPE] * scale).astype(BF16)
        q_ref[:, o + MLA_NOPE:o + MLA_QKW] = (_rope128(qf[:, o + MLA_NOPE:o + MLA_QKW], cos_t, sin_t) * scale).astype(BF16)
        k_ref[:, o:o + MLA_NOPE] = k_nope[:, h * MLA_NOPE:(h + 1) * MLA_NOPE].astype(BF16)
        k_ref[:, o + MLA_NOPE:o + MLA_QKW] = kpe_b


def _mla_pre(proj, cos_t, sin_t, g_q, g_kv, w_uq, w_uk, w_uv_t):
    rows = proj.shape[0]
    tm = _tile(rows, 256, LANE)
    qkw = MLA_HEADS * MLA_QKW
    vw = MLA_HEADS * MLA_V
    row = lambda i: (i, 0)
    const = lambda i: (0, 0)
    return pl.pallas_call(
        _mla_pre_body,
        grid=(rows // tm,),
        in_specs=[
            pl.BlockSpec((tm, MLA_Q_LORA), lambda i: (i, AB_CQ // MLA_Q_LORA)),
            pl.BlockSpec((tm, MLA_KV_LORA), lambda i: (i, AB_CKV // MLA_KV_LORA)),
            pl.BlockSpec((tm, LANE), lambda i: (i, AB_KPE // LANE)),
            pl.BlockSpec((tm, LANE), row),
            pl.BlockSpec((tm, LANE), row),
            pl.BlockSpec((1, MLA_Q_LORA), const),
            pl.BlockSpec((1, MLA_KV_LORA), const),
            pl.BlockSpec((MLA_Q_LORA, qkw), const),
            pl.BlockSpec((MLA_KV_LORA, MLA_HEADS * MLA_NOPE), const),
            pl.BlockSpec((vw, MLA_KV_LORA), const),
        ],
        out_specs=[
            pl.BlockSpec((tm, qkw), row),
            pl.BlockSpec((tm, qkw), row),
            pl.BlockSpec((vw, tm), lambda i: (0, i)),
            pl.BlockSpec((tm, MLA_KV_LORA), row),
            pl.BlockSpec((tm, LANE), row),
        ],
        out_shape=[
            jax.ShapeDtypeStruct((rows, qkw), BF16),
            jax.ShapeDtypeStruct((rows, qkw), BF16),
            jax.ShapeDtypeStruct((vw, rows), BF16),
            jax.ShapeDtypeStruct((rows, MLA_KV_LORA), F32),
            jax.ShapeDtypeStruct((rows, LANE), F32),
        ],
        compiler_params=_params("parallel"),
        name="mla_pre",
    )(proj, proj, proj, cos_t, sin_t, g_q, g_kv, w_uq, w_uk, w_uv_t)


def _flash_body(q_ref, k_ref, vt_ref, dst_ref, o_ref, m_ref, l_ref, acc_ref, *, blk):
    del dst_ref
    qi = pl.program_id(1)
    n_heads = q_ref.shape[1] // MLA_QKW
    m_ref[...] = jnp.full(m_ref.shape, -jnp.inf, F32)
    l_ref[...] = jnp.zeros(l_ref.shape, F32)
    acc_ref[...] = jnp.zeros(acc_ref.shape, F32)

    def block(ki, masked):
        start = pl.multiple_of(ki * blk, blk)
        for h in range(n_heads):
            qk = slice(h * MLA_QKW, (h + 1) * MLA_QKW)
            st = _dot_nt(k_ref[pl.ds(start, blk), qk], q_ref[:, qk])
            if masked:
                k_chunk = lax.broadcasted_iota(jnp.int32, (blk, blk), 0) // CHUNK
                q_chunk = lax.broadcasted_iota(jnp.int32, (blk, blk), 1) // CHUNK
                st = jnp.where(k_chunk <= q_chunk, st, -jnp.inf)
            m_old = m_ref[h]
            m_new = jnp.maximum(m_old, jnp.max(st, axis=0, keepdims=True))
            alpha = jnp.exp2(m_old - m_new)
            p = jnp.exp2(st - m_new)
            l_ref[h] = alpha * l_ref[h] + jnp.sum(p, axis=0, keepdims=True)
            vt = vt_ref[h * MLA_V:(h + 1) * MLA_V, pl.ds(start, blk)]
            acc_ref[h] = alpha * acc_ref[h] + _dot(vt, p.astype(BF16))
            m_ref[h] = m_new

    def interior(ki, carry):
        block(ki, False)
        return carry

    lax.fori_loop(0, qi, interior, 0)
    block(qi, True)
    for h in range(n_heads):
        o_ref[:, h * MLA_V:(h + 1) * MLA_V] = (acc_ref[h] / l_ref[h]).T.astype(o_ref.dtype)


def _flash(q, k, vt, seq_len, tile, dst):
    blk = _tile(seq_len, tile, LANE)
    assert blk % CHUNK == 0
    hp = MLA_HEADS_PER_STEP
    return pl.pallas_call(
        functools.partial(_flash_body, blk=blk),
        grid=(MLA_HEADS // hp, seq_len // blk),
        in_specs=[
            pl.BlockSpec((blk, hp * MLA_QKW), lambda h, qi: (qi, h)),
            pl.BlockSpec((seq_len, hp * MLA_QKW), lambda h, qi: (0, h), pipeline_mode=pl.Buffered(1)),
            pl.BlockSpec((hp * MLA_V, seq_len), lambda h, qi: (h, 0), pipeline_mode=pl.Buffered(1)),
            pl.BlockSpec(memory_space=pl.ANY),
        ],
        out_specs=pl.BlockSpec((blk, hp * MLA_V), lambda h, qi: (qi, h)),
        out_shape=jax.ShapeDtypeStruct(dst.shape, dst.dtype),
        scratch_shapes=[pltpu.VMEM((hp, 1, blk), F32), pltpu.VMEM((hp, 1, blk), F32),
                        pltpu.VMEM((hp, MLA_V, blk), F32)],
        input_output_aliases={3: 0},
        compiler_params=_params("parallel", "arbitrary"),
        name="mla_flash",
    )(q, k, vt, dst)


def _mla_hist_body(q_ref, kn_ref, ckvn_ref, ckv_ref, kpe_ref, wuk_ref, wuv_ref, dst_ref, o_ref, kpe_pad_ref, *, past_len):
    del dst_ref
    t_new = q_ref.shape[0]
    v_new = _dot(ckvn_ref[...].astype(BF16), wuv_ref[...]).astype(BF16)
    ckv_b = ckv_ref[0].astype(BF16)
    k_past = _dot(ckv_b, wuk_ref[...]).astype(BF16)
    v_past = _dot(ckv_b, wuv_ref[...]).astype(BF16)
    kpe_pad_ref[:, :MLA_ROPE] = kpe_ref[0].astype(BF16)
    kpe_pad_ref[:, MLA_ROPE:] = jnp.zeros((past_len, LANE - MLA_ROPE), BF16)
    kpe_past = kpe_pad_ref[...]

    q_chunk = (past_len + lax.broadcasted_iota(jnp.int32, (t_new, past_len), 0)) // CHUNK
    mask_past = lax.broadcasted_iota(jnp.int32, (t_new, past_len), 1) // CHUNK <= q_chunk
    q_chunk_n = (past_len + lax.broadcasted_iota(jnp.int32, (t_new, t_new), 0)) // CHUNK
    mask_new = (past_len + lax.broadcasted_iota(jnp.int32, (t_new, t_new), 1)) // CHUNK <= q_chunk_n

    for h in range(MLA_HEADS):
        o = h * MLA_QKW
        q_nope = q_ref[:, o:o + MLA_NOPE]
        q_pe = q_ref[:, o + MLA_NOPE:o + MLA_QKW]
        s_past = _dot_nt(q_nope, k_past[:, h * MLA_NOPE:(h + 1) * MLA_NOPE]) + _dot_nt(q_pe, kpe_past)
        s_new = _dot_nt(q_ref[:, o:o + MLA_QKW], kn_ref[:, o:o + MLA_QKW])
        s_past = jnp.where(mask_past, s_past, -jnp.inf)
        s_new = jnp.where(mask_new, s_new, -jnp.inf)
        m = jnp.maximum(jnp.max(s_past, axis=-1, keepdims=True), jnp.max(s_new, axis=-1, keepdims=True))
        p_past = jnp.exp2(s_past - m)
        p_new = jnp.exp2(s_new - m)
        denom = jnp.sum(p_past, axis=-1, keepdims=True) + jnp.sum(p_new, axis=-1, keepdims=True)
        pv = _dot(p_past.astype(BF16), v_past[:, h * MLA_V:(h + 1) * MLA_V])
        pv = pv + _dot(p_new.astype(BF16), v_new[:, h * MLA_V:(h + 1) * MLA_V])
        o_ref[:, h * MLA_V:(h + 1) * MLA_V] = (pv / denom).astype(o_ref.dtype)


def _mla_hist(q, k, ckv_n, row0, n_seq, t_new, ckv_past, kpe_past, w_uk, w_uv, dst):
    past_len = ckv_past.shape[1]
    base = row0 // t_new
    qkw = MLA_HEADS * MLA_QKW
    vw = MLA_HEADS * MLA_V
    row = lambda b: (base + b, 0)
    const = lambda b: (0, 0)
    return pl.pallas_call(
        functools.partial(_mla_hist_body, past_len=past_len),
        grid=(n_seq,),
        in_specs=[
            pl.BlockSpec((t_new, qkw), row),
            pl.BlockSpec((t_new, qkw), row),
            pl.BlockSpec((t_new, MLA_KV_LORA), row),
            pl.BlockSpec((1, past_len, MLA_KV_LORA), lambda b: (b, 0, 0)),
            pl.BlockSpec((1, past_len, MLA_ROPE), lambda b: (b, 0, 0)),
            pl.BlockSpec((MLA_KV_LORA, MLA_HEADS * MLA_NOPE), const),
            pl.BlockSpec((MLA_KV_LORA, vw), const),
            pl.BlockSpec(memory_space=pl.ANY),
        ],
        out_specs=pl.BlockSpec((t_new, vw), row),
        out_shape=jax.ShapeDtypeStruct(dst.shape, dst.dtype),
        scratch_shapes=[pltpu.VMEM((past_len, LANE), BF16)],
        input_output_aliases={7: 0},
        compiler_params=_params("parallel"),
        name="mla_hist",
    )(q, k, ckv_n, ckv_past, kpe_past, w_uk, w_uv, dst)


SSD_L = 128
SSD_GROUPS_PER_STEP = 8


def _conv_silu(x_ref, buf_ref, w_ref, b_ref, ext_ref, first):
    L = x_ref.shape[0]

    @pl.when(first)
    def _():
        ext_ref[8 - (SSD_CONV - 1):8, :] = buf_ref[0]

    @pl.when(jnp.logical_not(first))
    def _():
        ext_ref[0:8, :] = ext_ref[L:L + 8, :]

    ext_ref[8:8 + L, :] = x_ref[...]
    y = b_ref[...]
    for j in range(SSD_CONV):
        y = y + ext_ref[8 - (SSD_CONV - 1) + j:8 - (SSD_CONV - 1) + j + L, :] * w_ref[j:j + 1, :]
    return _silu(y)


def _pad_rows(x, n):
    if x.shape[0] == n:
        return x
    return jnp.concatenate([x, jnp.zeros((n - x.shape[0], x.shape[1]), x.dtype)], axis=0)


def _ssd_body(z_ref, x_ref, bm_ref, cm_ref, dt_ref, bufx_ref, bufb_ref, bufc_ref, wx_ref, wb_ref, wc_ref,
              bx_ref, bb_ref, bc_ref, dtb_ref, alog_ref, dsk_ref, gn_ref, s0_ref, dst_ref,
              y_ref, s_out_ref, s_ref, extx_ref, extb_ref, extc_ref):
    del dst_ref
    c = pl.program_id(2)
    first = c == 0
    L = x_ref.shape[0]
    LS = SSD_L
    n_grp = x_ref.shape[1] // SSD_GW

    @pl.when(first)
    def _():
        for p in range(n_grp):
            s_ref[p] = s0_ref[0, p * SSD_REP:(p + 1) * SSD_REP].reshape(SSD_GW, SSD_N).T

    xs_all = _conv_silu(x_ref, bufx_ref, wx_ref, bx_ref, extx_ref, first)
    bm_all = _conv_silu(bm_ref, bufb_ref, wb_ref, bb_ref, extb_ref, first)
    cm_all = _conv_silu(cm_ref, bufc_ref, wc_ref, bc_ref, extc_ref, first)

    row = lax.broadcasted_iota(jnp.int32, (LS, LS), 0)
    col = lax.broadcasted_iota(jnp.int32, (LS, LS), 1)
    tri = jnp.where(row >= col, 1.0, 0.0).astype(BF16)
    erow = lax.broadcasted_iota(jnp.int32, (LANE, SSD_GW), 0)
    ecol = lax.broadcasted_iota(jnp.int32, (LANE, SSD_GW), 1) // SSD_HEADDIM
    expand = jnp.where(erow == ecol, 1.0, 0.0).astype(BF16)
    causal = lax.broadcasted_iota(jnp.int32, (L, LS), 0) >= lax.broadcasted_iota(jnp.int32, (L, LS), 1)
    lane_half = lax.broadcasted_iota(jnp.int32, (LS, LANE), 1) < SSD_HEADDIM

    new_states = []
    for p in range(n_grp):
        gw = slice(p * SSD_GW, (p + 1) * SSD_GW)
        gn = slice(p * SSD_N, (p + 1) * SSD_N)
        xs = xs_all[:, gw]
        cm_b = cm_all[:, gn].astype(BF16)
        dt = _softplus(dt_ref[:, p * LANE:(p + 1) * LANE] + dtb_ref[p])
        xs_p = _pad_rows(xs, LS)
        bm_p = _pad_rows(bm_all[:, gn], LS).astype(BF16)
        dt_p = _pad_rows(dt, LS)
        da = dt_p * (-jnp.exp(alog_ref[p]))

        cum = _dot_exact_rhs(tri, da)
        cum_t = cum.T
        dt_t = dt_p.T
        cum_q = cum[:L]
        last = cum[LS - 1:LS, :]
        decays = jnp.concatenate([jnp.exp(cum_q), jnp.exp(last - cum) * dt_p,
                                  jnp.broadcast_to(jnp.exp(last), (SUBLANE, LANE))], axis=0)
        decays = _dot_exact_lhs(decays, expand)
        e_cum = decays[:L]
        w_dec = decays[L:L + LS]
        e_last = decays[L + LS:L + LS + 1]
        cb = _dot_nt(cm_b, bm_p)

        def head_weights(r):
            diff = cum_q[:, r:r + 1] - cum_t[r:r + 1, :]
            return (cb * jnp.exp(jnp.where(causal, diff, -jnp.inf)) * dt_t[r:r + 1, :]).astype(BF16)

        y_parts = []
        for q in range(SSD_REP // 2):
            w_pair = jnp.concatenate([head_weights(2 * q), head_weights(2 * q + 1)], axis=1)
            x_pair = xs_p[:, q * LANE:(q + 1) * LANE]
            x_bd = jnp.concatenate([jnp.where(lane_half, x_pair, 0.0), jnp.where(lane_half, 0.0, x_pair)], axis=0)
            y_parts.append(_dot(w_pair, x_bd.astype(BF16)))
        y = jnp.concatenate(y_parts, axis=1)

        s_old = s_ref[p]
        y = y + _dot(cm_b, s_old.astype(BF16)) * e_cum
        s_new = s_old * e_last + _dot_tn(bm_p, (xs_p * w_dec).astype(BF16))
        s_ref[p] = s_new
        new_states.append(s_new)

        y = y + xs * dsk_ref[:, gw]
        y = y * _silu(z_ref[:, gw])
        y_ref[:, gw] = _rms(y, gn_ref[:, gw]).astype(y_ref.dtype)

    @pl.when(c == pl.num_programs(2) - 1)
    def _():
        for p in range(n_grp):
            s_out_ref[0, p * SSD_REP:(p + 1) * SSD_REP] = new_states[p].T.reshape(SSD_REP, SSD_HEADDIM, SSD_N)


def _ssd(proj, row0, n_seq, seq_len, chunk, conv_buf, s0, w_conv, b_conv, dt_bias, a_log, d_skip, g_norm, dst):
    d_inner = SSD_GROUPS * SSD_GW
    L = chunk
    n_chunks = seq_len // L
    base = row0 // L
    bn = SSD_GROUPS * SSD_N
    gp = SSD_GROUPS_PER_STEP
    gw, gn, gl = gp * SSD_GW, gp * SSD_N, gp * LANE
    off_x, off_b, off_c, off_dt = d_inner, 2 * d_inner, 2 * d_inner + bn, 2 * d_inner + 2 * bn

    def rmap(col_blocks_off):
        return lambda b, g, c: (base + b * n_chunks + c, col_blocks_off + g)

    def cmap(col_blocks_off):
        return lambda b, g, c: (0, col_blocks_off + g)

    nb = SSD_CONV - 1
    in_specs = [
        pl.BlockSpec((L, gw), rmap(0)),
        pl.BlockSpec((L, gw), rmap(off_x // gw)),
        pl.BlockSpec((L, gn), rmap(off_b // gn)),
        pl.BlockSpec((L, gn), rmap(off_c // gn)),
        pl.BlockSpec((L, gl), rmap(off_dt // gl)),
        pl.BlockSpec((1, nb, gw), lambda b, g, c: (b, 0, g)),
        pl.BlockSpec((1, nb, gn), lambda b, g, c: (b, 0, d_inner // gn + g)),
        pl.BlockSpec((1, nb, gn), lambda b, g, c: (b, 0, (d_inner + bn) // gn + g)),
        pl.BlockSpec((SSD_CONV, gw), cmap(0)),
        pl.BlockSpec((SSD_CONV, gn), cmap(d_inner // gn)),
        pl.BlockSpec((SSD_CONV, gn), cmap((d_inner + bn) // gn)),
        pl.BlockSpec((1, gw), cmap(0)),
        pl.BlockSpec((1, gn), cmap(d_inner // gn)),
        pl.BlockSpec((1, gn), cmap((d_inner + bn) // gn)),
        pl.BlockSpec((gp, 1, LANE), lambda b, g, c: (g, 0, 0)),
        pl.BlockSpec((gp, 1, LANE), lambda b, g, c: (g, 0, 0)),
        pl.BlockSpec((1, gw), cmap(0)),
        pl.BlockSpec((1, gw), cmap(0)),
        pl.BlockSpec((1, gp * SSD_REP, SSD_HEADDIM, SSD_N), lambda b, g, c: (b, g, 0, 0)),
        pl.BlockSpec(memory_space=pl.ANY),
    ]
    args = [proj] * 5 + [conv_buf] * 3 + [w_conv] * 3 + [b_conv] * 3 + [dt_bias, a_log, d_skip, g_norm, s0, dst]
    return pl.pallas_call(
        _ssd_body,
        grid=(n_seq, SSD_GROUPS // gp, n_chunks),
        in_specs=in_specs,
        out_specs=[
            pl.BlockSpec((L, gw), rmap(0)),
            pl.BlockSpec((1, gp * SSD_REP, SSD_HEADDIM, SSD_N), lambda b, g, c: (b, g, 0, 0)),
        ],
        out_shape=[
            jax.ShapeDtypeStruct(dst.shape, dst.dtype),
            jax.ShapeDtypeStruct((n_seq, SSD_GROUPS * SSD_REP, SSD_HEADDIM, SSD_N), F32),
        ],
        scratch_shapes=[
            pltpu.VMEM((gp, SSD_N, SSD_GW), F32),
            pltpu.VMEM((L + 8, gw), F32),
            pltpu.VMEM((L + 8, gn), F32),
            pltpu.VMEM((L + 8, gn), F32),
        ],
        input_output_aliases={len(args) - 1: 0},
        compiler_params=_params("parallel", "parallel", "arbitrary"),
        name="ssd",
    )(*args)


def _rot_cols(w):
    half = w.shape[-1] // 2
    return jnp.concatenate([-w[..., half:], w[..., :half]], axis=-1)


def _prep_ab_in(w):
    d = w.shape[0]
    o = 0
    parts = {}
    for name, size in (("q", GLA_QK), ("k", GLA_QK), ("v", GLA_VW), ("gr", GLA_GATE_RANK), ("og", GLA_VW),
                       ("cq", MLA_Q_LORA), ("ckv", MLA_KV_LORA), ("kpe", MLA_ROPE)):
        parts[name] = w[:, o:o + size]
        o += size
    zeros = jnp.zeros((d, LANE - GLA_GATE_RANK), w.dtype)
    out = jnp.concatenate([parts["q"], parts["k"], parts["v"], parts["og"], parts["cq"], parts["ckv"],
                           parts["kpe"], _rot_cols(parts["kpe"]), parts["gr"], zeros], axis=1)
    assert out.shape[1] == AB_W
    return out.astype(BF16)


def _prep_uq(w):
    w = w.reshape(w.shape[0], MLA_HEADS, MLA_NOPE + MLA_ROPE)
    pe = w[..., MLA_NOPE:]
    out = jnp.concatenate([w[..., :MLA_NOPE], pe, _rot_cols(pe)], axis=-1)
    return out.reshape(w.shape[0], MLA_HEADS * MLA_QKW).astype(BF16)


def _prep_ssd_in(w, d_inner):
    d = w.shape[0]
    n_heads = SSD_GROUPS * SSD_REP
    main = w[:, :2 * d_inner + 2 * SSD_GROUPS * SSD_N]
    dt = w[:, 2 * d_inner + 2 * SSD_GROUPS * SSD_N:].reshape(d, SSD_GROUPS, SSD_REP)
    assert dt.shape[1] * dt.shape[2] == n_heads
    dt = jnp.concatenate([dt, jnp.zeros((d, SSD_GROUPS, LANE - SSD_REP), w.dtype)], axis=-1)
    return jnp.concatenate([main, dt.reshape(d, SSD_GROUPS * LANE)], axis=1).astype(BF16)


def _head_table(v):
    v = v.reshape(SSD_GROUPS, 1, SSD_REP)
    return jnp.concatenate([v, jnp.zeros((SSD_GROUPS, 1, LANE - SSD_REP), v.dtype)], axis=-1)


def _rope_tables(pos):
    half = MLA_ROPE // 2
    inv = ROPE_THETA ** (-jnp.arange(half, dtype=F32) / half)
    ang = pos.astype(F32)[:, None] * inv[None, :]
    zeros = jnp.zeros((pos.shape[0], LANE - MLA_ROPE), F32)
    cos = jnp.cos(ang)
    sin = jnp.sin(ang)
    return jnp.concatenate([cos, cos, zeros], axis=1), jnp.concatenate([sin, sin, zeros], axis=1)


def kernel(x_prompt, x_sample, cache_mla_ckv, cache_mla_kpe, state_gla, state_ssd, state_ssd_conv, g_ffn1, w_ffn1_gate, w_ffn1_up, w_ffn1_down, g_mix, g_ffn2, w_ffn2_gate, w_ffn2_up, w_ffn2_down, w_ab_in, w_gla_gate_up, b_gla_gate, g_gla_norm, g_mla_q_norm, w_mla_uq, g_mla_kv_norm, w_mla_uk, w_mla_uv, w_ab_out, w_ssd_in, w_ssd_conv, b_ssd_conv, ssd_dt_bias, ssd_a_log, ssd_d, g_ssd_norm, w_ssd_out, g_final):
    n_p, t_p, d = x_prompt.shape
    n_s, t_s, _ = x_sample.shape
    assert n_p == 1
    depth = g_ffn1.shape[0]
    past_len = cache_mla_ckv.shape[2]
    rows_p = n_p * t_p
    d_inner = w_ssd_out.shape[1]
    assert d_inner == SSD_GROUPS * SSD_GW

    x = jnp.concatenate([x_prompt.reshape(rows_p, d), x_sample.reshape(n_s * t_s, d)], axis=0)
    rows = rows_p + n_s * t_s
    pos = jnp.concatenate([jnp.arange(t_p, dtype=jnp.int32),
                           jnp.tile(past_len + jnp.arange(t_s, dtype=jnp.int32), n_s)])
    cos_t, sin_t = _rope_tables(pos)
    row2 = lambda v: v.reshape(1, -1)

    outs = {k: [] for k in ("p_ckv", "p_kpe", "p_gla", "p_ssd", "p_conv", "s_ckv", "s_kpe", "s_gla", "s_ssd", "s_conv")}
    for layer in range(depth):
        i = layer // 2
        x = _ffn(x, row2(g_ffn1[layer]), w_ffn1_gate, w_ffn1_up, w_ffn1_down, layer)
        if layer % 2 == 0:
            proj = _norm_proj(x, row2(g_mix[layer]), _prep_ab_in(w_ab_in[i]), AB_W // 2)
            w_gate = jnp.concatenate([w_gla_gate_up[i], jnp.zeros((LANE - GLA_GATE_RANK, GLA_QK), F32)], axis=0).astype(BF16)
            gla_args = (w_gate, row2(b_gla_gate[i]), row2(g_gla_norm[i]))
            gla_chunk = min(CHUNK, t_p)
            o_gla, p_gla = _gla(proj, 0, n_p, t_p, gla_chunk, min(4, t_p // gla_chunk),
                                jnp.zeros((n_p, GLA_HEADS, GLA_DK, GLA_DV), F32), *gla_args,
                                dst=jnp.zeros((rows, GLA_VW), BF16))
            o_gla, s_gla = _gla(proj, rows_p, n_s, t_s, t_s, 1, state_gla[i], *gla_args, dst=o_gla)

            w_uk = w_mla_uk[i].astype(BF16)
            w_uv = w_mla_uv[i].astype(BF16)
            q, k, vt, ckv_n, kpe_r = _mla_pre(proj, cos_t, sin_t, row2(g_mla_q_norm[i]), row2(g_mla_kv_norm[i]),
                                              _prep_uq(w_mla_uq[i]), w_uk, w_uv.T)
            o_mla = _flash(q, k, vt, t_p, 512, jnp.zeros((rows, MLA_HEADS * MLA_V), BF16))
            o_mla = _mla_hist(q, k, ckv_n, rows_p, n_s, t_s, cache_mla_ckv[i], cache_mla_kpe[i], w_uk, w_uv, o_mla)

            w_out = w_ab_out[i].astype(BF16)
            x = _out_proj(x, [o_gla, o_mla], [w_out[:GLA_VW], w_out[GLA_VW:]])
            outs["p_ckv"].append(ckv_n[:rows_p].reshape(n_p, t_p, MLA_KV_LORA))
            outs["s_ckv"].append(ckv_n[rows_p:].reshape(n_s, t_s, MLA_KV_LORA))
            outs["p_kpe"].append(kpe_r[:rows_p, :MLA_ROPE].reshape(n_p, t_p, MLA_ROPE))
            outs["s_kpe"].append(kpe_r[rows_p:, :MLA_ROPE].reshape(n_s, t_s, MLA_ROPE))
            outs["p_gla"].append(p_gla)
            outs["s_gla"].append(s_gla)
        else:
            n_heads = SSD_GROUPS * SSD_REP
            conv_ch = d_inner + 2 * SSD_GROUPS * SSD_N
            proj = _norm_proj(x, row2(g_mix[layer]), _prep_ssd_in(w_ssd_in[i], d_inner),
                              (2 * d_inner + 2 * SSD_GROUPS * SSD_N + SSD_GROUPS * LANE) // 8)
            ssd_args = (w_ssd_conv[i], row2(b_ssd_conv[i]), _head_table(ssd_dt_bias[i]), _head_table(ssd_a_log[i]),
                        row2(jnp.repeat(ssd_d[i], SSD_HEADDIM)), row2(g_ssd_norm[i]))
            y, p_ssd = _ssd(proj, 0, n_p, t_p, min(SSD_L, t_p), jnp.zeros((n_p, SSD_CONV - 1, conv_ch), F32),
                            jnp.zeros((n_p, n_heads, SSD_HEADDIM, SSD_N), F32), *ssd_args,
                            dst=jnp.zeros((rows, d_inner), BF16))
            y, s_ssd = _ssd(proj, rows_p, n_s, t_s, t_s, state_ssd_conv[i], state_ssd[i], *ssd_args, dst=y)
            x = _out_proj(x, [y], [w_ssd_out[i].astype(BF16)])
            nb = SSD_CONV - 1
            outs["p_conv"].append(jnp.stack([
                lax.slice(proj, (b * t_p + t_p - nb, d_inner), (b * t_p + t_p, d_inner + conv_ch)) for b in range(n_p)]))
            outs["s_conv"].append(jnp.stack([
                lax.slice(proj, (rows_p + b * t_s + t_s - nb, d_inner), (rows_p + b * t_s + t_s, d_inner + conv_ch))
                for b in range(n_s)]))
            outs["p_ssd"].append(p_ssd)
            outs["s_ssd"].append(s_ssd)
        x = _ffn(x, row2(g_ffn2[layer]), w_ffn2_gate, w_ffn2_up, w_ffn2_down, layer,
                 g_final=row2(g_final) if layer == depth - 1 else None)

    y_prompt = x[:rows_p].reshape(n_p, t_p, d)
    y_sample = x[rows_p:].reshape(n_s, t_s, d)
    st = {k: jnp.stack(v) for k, v in outs.items()}
    return (y_prompt, y_sample, st["p_ckv"], st["p_kpe"], st["p_gla"], st["p_ssd"], st["p_conv"],
            st["s_ckv"], st["s_kpe"], st["s_gla"], st["s_ssd"], st["s_conv"])
```
